```python
import functools
import jax, jax.numpy as jnp
from jax import lax
import numpy as np

D_MODEL = 1024
BATCH = 1
SEQ = 16384
DEPTH = 2
DEC_BATCH = 128
DEC_SEQ = 4
PAST_LEN = 16384
PAGE_SIZE = 128

N_HEADS = 8
QK_NOPE_DIM = 64
QK_ROPE_DIM = 32
QK_HEAD_DIM = QK_NOPE_DIM + QK_ROPE_DIM
V_HEAD_DIM = 64
Q_LORA_RANK = 384
KV_LORA_RANK = 256
ROPE_THETA = 10000.0
MLA_SCALE = QK_HEAD_DIM ** -0.5
Q_BLOCK = 128
CHUNK = 128
N_GROUPS = 4
GROUP_DIM = 128
SGU_DIM = N_GROUPS * GROUP_DIM
N_MEM = 256
MEM_HEADS = 4
MEM_HEAD_DIM = 128
MEM_DIM = MEM_HEADS * MEM_HEAD_DIM
MEM_SCALE = MEM_HEAD_DIM ** -0.5
N_BRANCH = 3
BRANCH_DIM = 512
D_FF = ((8 * D_MODEL // 3 + 255) // 256) * 256
ALPHA = (2 * DEPTH) ** 0.25
BETA = (8 * DEPTH) ** -0.25
LN_EPS = 1e-5
RMS_EPS = 1e-6
OFF_CQ = 0
OFF_CKV = OFF_CQ + Q_LORA_RANK
OFF_KR = OFF_CKV + KV_LORA_RANK
OFF_U = OFF_KR + QK_ROPE_DIM
OFF_V = OFF_U + SGU_DIM
OFF_XQ = OFF_V + SGU_DIM
OFF_G = OFF_XQ + MEM_DIM
D_IN = OFF_G + N_BRANCH * D_MODEL

kernel_name = "hybrid_mla_chunkmlp_memxattn_decode_step"


def layer_norm(x, g, b):
    xf = x.astype(jnp.float32)
    mu = jnp.mean(xf, axis=-1, keepdims=True)
    var = jnp.mean(jnp.square(xf - mu), axis=-1, keepdims=True)
    return ((xf - mu) * lax.rsqrt(var + LN_EPS) * g.astype(jnp.float32) + b.astype(jnp.float32)).astype(x.dtype)


def rms_norm(x, g):
    xf = x.astype(jnp.float32)
    ms = jnp.mean(jnp.square(xf), axis=-1, keepdims=True)
    return (xf * lax.rsqrt(ms + RMS_EPS) * g.astype(jnp.float32)).astype(x.dtype)


def apply_rope(x, pos):
    half = x.shape[-1] // 2
    inv_freq = jnp.power(ROPE_THETA, -jnp.arange(half, dtype=jnp.float32) / half)
    ang = pos.astype(jnp.float32)[:, None] * inv_freq[None, :]
    shp = (pos.shape[0],) + (1,) * (x.ndim - 3) + (half,)
    cos = jnp.cos(ang).reshape(shp)
    sin = jnp.sin(ang).reshape(shp)
    x1 = x[..., :half].astype(jnp.float32)
    x2 = x[..., half:].astype(jnp.float32)
    return jnp.concatenate([x1 * cos - x2 * sin, x2 * cos + x1 * sin], axis=-1).astype(x.dtype)


def mla_prompt(q_nope, q_rope, c_kv, k_rope, w_uk, w_uv):
    B, S = c_kv.shape[:2]
    k_nope = jnp.einsum('bsr,rhd->bshd', c_kv, w_uk)
    v = jnp.einsum('bsr,rhd->bshd', c_kv, w_uv)
    k_pos = jnp.arange(S)

    def block(i):
        start = i * Q_BLOCK
        qn = lax.dynamic_slice_in_dim(q_nope, start, Q_BLOCK, axis=1)
        qr = lax.dynamic_slice_in_dim(q_rope, start, Q_BLOCK, axis=1)
        s = (jnp.einsum('bqhd,bkhd->bhqk', qn, k_nope)
             + jnp.einsum('bqhd,bkd->bhqk', qr, k_rope)).astype(jnp.float32) * MLA_SCALE
        q_pos = start + jnp.arange(Q_BLOCK)
        s = jnp.where(k_pos[None, :] <= q_pos[:, None], s, -jnp.inf)
        p = jax.nn.softmax(s, axis=-1).astype(v.dtype)
        return jnp.einsum('bhqk,bkhd->bqhd', p, v)

    out = lax.map(block, jnp.arange(S // Q_BLOCK))
    return jnp.moveaxis(out, 0, 1).reshape(B, S, N_HEADS * V_HEAD_DIM)


def mla_sample(q_nope, q_rope, c_new, kr_new, w_uk, w_uv, c_past, kr_past):
    B, T = c_new.shape[:2]
    P = c_past.shape[1]
    q_lat = jnp.einsum('bqhd,rhd->bqhr', q_nope, w_uk)
    s_past = jnp.einsum('bqhr,bkr->bhqk', q_lat, c_past) + jnp.einsum('bqhd,bkd->bhqk', q_rope, kr_past)
    s_new = jnp.einsum('bqhr,bkr->bhqk', q_lat, c_new) + jnp.einsum('bqhd,bkd->bhqk', q_rope, kr_new)
    s = jnp.concatenate([s_past, s_new], axis=-1).astype(jnp.float32) * MLA_SCALE
    allowed = jnp.concatenate([jnp.ones((T, P), dtype=bool), jnp.tril(jnp.ones((T, T), dtype=bool))], axis=1)
    s = jnp.where(allowed, s, -jnp.inf)
    p = jax.nn.softmax(s, axis=-1).astype(c_new.dtype)
    o_lat = (jnp.einsum('bhqk,bkr->bqhr', p[..., :P], c_past)
             + jnp.einsum('bhqk,bkr->bqhr', p[..., P:], c_new))
    return jnp.einsum('bqhr,rhd->bqhd', o_lat, w_uv).reshape(B, T, N_HEADS * V_HEAD_DIM)


def chunk_mix(u, v, w_s, b_s):
    B, S, _ = v.shape
    L = min(S, CHUNK)
    n = S // L
    vg = v.reshape(B, n, L, N_GROUPS, GROUP_DIM)
    mask = jnp.tril(jnp.ones((L, L), dtype=bool))
    w = jnp.where(mask[None], w_s[:, :L, :L], 0)
    mixed = jnp.einsum('gij,bnjgc->bnigc', w, vg) + jnp.transpose(b_s[:, :L])[None, None, :, :, None]
    return u * mixed.reshape(B, S, SGU_DIM)


def mem_attention(xq, mk, mv):
    B, S = xq.shape[:2]
    s = jnp.einsum('bqhd,bkhd->bhqk', xq, mk).astype(jnp.float32) * MEM_SCALE
    p = jax.nn.softmax(s, axis=-1).astype(mv.dtype)
    return jnp.einsum('bhqk,bkhd->bqhd', p, mv).reshape(B, S, MEM_DIM)


def layer_forward(x, pos, attend, mk, mv, w_in, b_gate, q_norm_g, kv_norm_g, w_uq, w_uk, w_uv,
                  sgu_ln_g, sgu_ln_b, w_s, b_s, w_br, w_o, ln1_g, ln1_b, w_ff1, w_ff3, w_ff2, ln2_g, ln2_b):
    B, S, _ = x.shape
    h = x @ w_in
    c_q = rms_norm(h[..., OFF_CQ:OFF_CKV], q_norm_g)
    c_kv = rms_norm(h[..., OFF_CKV:OFF_KR], kv_norm_g)
    k_rope = apply_rope(h[..., OFF_KR:OFF_U], pos)
    q = (c_q @ w_uq).reshape(B, S, N_HEADS, QK_HEAD_DIM)
    q_nope = q[..., :QK_NOPE_DIM]
    q_rope = apply_rope(q[..., QK_NOPE_DIM:], pos)
    uv = jax.nn.gelu(h[..., OFF_U:OFF_XQ])
    u = uv[..., :SGU_DIM]
    v = layer_norm(uv[..., SGU_DIM:], sgu_ln_g, sgu_ln_b)
    xq = h[..., OFF_XQ:OFF_G].reshape(B, S, MEM_HEADS, MEM_HEAD_DIM)
    gates = jax.nn.sigmoid(h[..., OFF_G:].reshape(B, S, N_BRANCH, D_MODEL) + b_gate)
    a_out = attend(q_nope, q_rope, c_kv, k_rope, w_uk, w_uv)
    s_out = chunk_mix(u, v, w_s, b_s)
    m_out = mem_attention(xq, mk, mv)
    branches = jnp.stack([a_out, s_out, m_out], axis=2)
    merged = jnp.sum(gates * jnp.einsum('bsie,ied->bsid', branches, w_br), axis=2)
    x = layer_norm(ALPHA * x + merged @ w_o, ln1_g, ln1_b)
    ff = (jax.nn.silu(x @ w_ff1) * (x @ w_ff3)) @ w_ff2
    x = layer_norm(ALPHA * x + ff, ln2_g, ln2_b)
    return x, c_kv, k_rope, v


def setup_inputs(seed: int = 0) -> dict:
    key = jax.random.key(seed)
    ks = iter(jax.random.split(key, 40))

    def nrm(shape, scale):
        return jax.random.normal(next(ks), shape, dtype=jnp.float32) * scale

    def gain(shape):
        return 1.0 + nrm(shape, 0.05)

    n_pages = PAST_LEN // PAGE_SIZE
    n_used = DEC_BATCH * n_pages
    n_pool = (5 * n_used) // 4
    page_table = jax.random.permutation(next(ks), n_pool)[:n_used].reshape(DEC_BATCH, n_pages).astype(jnp.int32)
    return {
        "x_prompt": nrm((BATCH, SEQ, D_MODEL), 1.0),
        "x_sample": nrm((DEC_BATCH, DEC_SEQ, D_MODEL), 1.0),
        "mem_prompt": nrm((BATCH, N_MEM, D_MODEL), 1.0),
        "cache_kv_latent": nrm((DEPTH, n_pool, PAGE_SIZE, KV_LORA_RANK), 1.0),
        "cache_k_rope": nrm((DEPTH, n_pool, PAGE_SIZE, QK_ROPE_DIM), 1.0),
        "cache_mem_k": nrm((DEPTH, DEC_BATCH, N_MEM, MEM_HEADS, MEM_HEAD_DIM), 1.0),
        "cache_mem_v": nrm((DEPTH, DEC_BATCH, N_MEM, MEM_HEADS, MEM_HEAD_DIM), BETA),
        "page_table": page_table,
        "w_in": nrm((DEPTH, D_MODEL, D_IN), D_MODEL ** -0.5),
        "b_gate": nrm((DEPTH, N_BRANCH, D_MODEL), 0.1),
        "q_norm_g": gain((DEPTH, Q_LORA_RANK)),
        "kv_norm_g": gain((DEPTH, KV_LORA_RANK)),
        "w_uq": nrm((DEPTH, Q_LORA_RANK, N_HEADS * QK_HEAD_DIM), Q_LORA_RANK ** -0.5),
        "w_uk": nrm((DEPTH, KV_LORA_RANK, N_HEADS, QK_NOPE_DIM), KV_LORA_RANK ** -0.5),
        "w_uv": nrm((DEPTH, KV_LORA_RANK, N_HEADS, V_HEAD_DIM), KV_LORA_RANK ** -0.5 * BETA),
        "sgu_ln_g": gain((DEPTH, SGU_DIM)),
        "sgu_ln_b": nrm((DEPTH, SGU_DIM), 0.02),
        "w_s": nrm((DEPTH, N_GROUPS, CHUNK, CHUNK), CHUNK ** -0.5),
        "b_s": gain((DEPTH, N_GROUPS, CHUNK)),
        "w_mk": nrm((DEPTH, D_MODEL, MEM_DIM), D_MODEL ** -0.5),
        "w_mv": nrm((DEPTH, D_MODEL, MEM_DIM), D_MODEL ** -0.5 * BETA),
        "w_br": nrm((DEPTH, N_BRANCH, BRANCH_DIM, D_MODEL), BRANCH_DIM ** -0.5 * BETA),
        "w_o": nrm((DEPTH, D_MODEL, D_MODEL), D_MODEL ** -0.5 * BETA),
        "ln1_g": gain((DEPTH, D_MODEL)),
        "ln1_b": nrm((DEPTH, D_MODEL), 0.02),
        "w_ff1": nrm((DEPTH, D_MODEL, D_FF), D_MODEL ** -0.5),
        "w_ff3": nrm((DEPTH, D_MODEL, D_FF), D_MODEL ** -0.5),
        "w_ff2": nrm((DEPTH, D_FF, D_MODEL), D_FF ** -0.5 * BETA),
        "ln2_g": gain((DEPTH, D_MODEL)),
        "ln2_b": nrm((DEPTH, D_MODEL), 0.02),
    }


def reference(x_prompt, x_sample, mem_prompt, cache_kv_latent, cache_k_rope, cache_mem_k, cache_mem_v, page_table,
              w_in, b_gate, q_norm_g, kv_norm_g, w_uq, w_uk, w_uv, sgu_ln_g, sgu_ln_b, w_s, b_s, w_mk, w_mv,
              w_br, w_o, ln1_g, ln1_b, w_ff1, w_ff3, w_ff2, ln2_g, ln2_b):
    B, S, _ = x_prompt.shape
    DB, T, _ = x_sample.shape
    n_pages = page_table.shape[1]
    past = n_pages * cache_kv_latent.shape[2]
    pos_p = jnp.arange(S, dtype=jnp.int32)
    pos_s = past + jnp.arange(T, dtype=jnp.int32)

    xp, xs = x_prompt, x_sample
    ckv_p, kr_p, mk_p_all, mv_p_all = [], [], [], []
    ckv_s, kr_s, v_s = [], [], []
    for l in range(DEPTH):
        lw = (w_in[l], b_gate[l], q_norm_g[l], kv_norm_g[l], w_uq[l], w_uk[l], w_uv[l], sgu_ln_g[l], sgu_ln_b[l],
              w_s[l], b_s[l], w_br[l], w_o[l], ln1_g[l], ln1_b[l], w_ff1[l], w_ff3[l], w_ff2[l], ln2_g[l], ln2_b[l])
        mk_p = jnp.einsum('bmd,de->bme', mem_prompt, w_mk[l]).reshape(B, N_MEM, MEM_HEADS, MEM_HEAD_DIM)
        mv_p = jnp.einsum('bmd,de->bme', mem_prompt, w_mv[l]).reshape(B, N_MEM, MEM_HEADS, MEM_HEAD_DIM)
        xp, c_new_p, k_new_p, _ = layer_forward(xp, pos_p, mla_prompt, mk_p, mv_p, *lw)
        ckv_p.append(c_new_p)
        kr_p.append(k_new_p)
        mk_p_all.append(mk_p)
        mv_p_all.append(mv_p)
        c_past = cache_kv_latent[l, page_table].reshape(DB, past, KV_LORA_RANK)
        kr_past = cache_k_rope[l, page_table].reshape(DB, past, QK_ROPE_DIM)
        attend_s = functools.partial(mla_sample, c_past=c_past, kr_past=kr_past)
        xs, c_new_s, k_new_s, v_new_s = layer_forward(xs, pos_s, attend_s, cache_mem_k[l], cache_mem_v[l], *lw)
        ckv_s.append(c_new_s)
        kr_s.append(k_new_s)
        v_s.append(v_new_s)

    return (xp, xs, jnp.stack(ckv_p), jnp.stack(kr_p), jnp.stack(mk_p_all), jnp.stack(mv_p_all),
            jnp.stack(ckv_s), jnp.stack(kr_s), jnp.stack(v_s))
```

```python
import functools

import numpy as np
import jax
import jax.numpy as jnp
from jax import lax
from jax.experimental import pallas as pl
from jax.experimental.pallas import tpu as pltpu

F32 = jnp.float32
BF16 = jnp.bfloat16

D_MODEL = 1024
N_HEADS = 8
QK_NOPE_DIM = 64
QK_ROPE_DIM = 32
QK_HEAD_DIM = QK_NOPE_DIM + QK_ROPE_DIM
V_HEAD_DIM = 64
Q_LORA_RANK = 384
KV_LORA_RANK = 256
ROPE_THETA = 10000.0
MLA_SCALE = QK_HEAD_DIM ** -0.5
CHUNK = 128
N_GROUPS = 4
GROUP_DIM = 128
SGU_DIM = N_GROUPS * GROUP_DIM
N_MEM = 256
MEM_HEADS = 4
MEM_HEAD_DIM = 128
MEM_DIM = MEM_HEADS * MEM_HEAD_DIM
MEM_SCALE = MEM_HEAD_DIM ** -0.5
N_BRANCH = 3
BRANCH_DIM = 512
LN_EPS = 1e-5
RMS_EPS = 1e-6
OFF_CQ = 0
OFF_CKV = OFF_CQ + Q_LORA_RANK
OFF_KR = OFF_CKV + KV_LORA_RANK
OFF_U = OFF_KR + QK_ROPE_DIM
OFF_V = OFF_U + SGU_DIM
OFF_XQ = OFF_V + SGU_DIM
OFF_G = OFF_XQ + MEM_DIM

LANES = 128
HEAD_SLAB = LANES
QK_SLAB = N_HEADS * HEAD_SLAB
ROPE_ALL = N_HEADS * QK_ROPE_DIM
P_CQ = 0
P_CKV = P_CQ + Q_LORA_RANK
P_KRA = P_CKV + KV_LORA_RANK
P_KRB = P_KRA + LANES
P_U = P_KRB + LANES
P_V = P_U + SGU_DIM
P_XQ = P_V + SGU_DIM
P_END = P_XQ + MEM_DIM
QCAT = 384
VMEM_LIMIT = 56 * 1024 * 1024


def _dot(a, b):
    return jnp.dot(a, b, preferred_element_type=F32)


def _dot_nt(a, b):
    return lax.dot_general(a, b, (((1,), (1,)), ((), ())), preferred_element_type=F32)


def _const_spec(shape):
    nd = len(shape)
    return pl.BlockSpec(shape, lambda *_: (0,) * nd, pipeline_mode=pl.Buffered(1))


def _params(n_axes):
    return pltpu.CompilerParams(dimension_semantics=("arbitrary",) * n_axes,
                                vmem_limit_bytes=VMEM_LIMIT)


def _layer_norm(x, g, b):
    mu = jnp.mean(x, axis=-1, keepdims=True)
    xc = x - mu
    var = jnp.mean(xc * xc, axis=-1, keepdims=True)
    return xc * lax.rsqrt(var + LN_EPS) * g + b


def _rms_norm(x, g):
    ms = jnp.mean(x * x, axis=-1, keepdims=True)
    return x * lax.rsqrt(ms + RMS_EPS) * g


def _proj_kernel(x_ref, cos_ref, sin_ref, w1_ref, qg_ref, kvg_ref, wq_ref, eq_ref, ek_ref, wuk_ref,
                 wuv_ref, lng_ref, lnb_ref, wmix_ref, bias_ref, *out_refs, chunk_len, with_kv, with_vln):
    out_refs = list(out_refs)
    ckv_ref, kr_ref, q_ref = out_refs[:3]
    rest = out_refs[3:]
    if with_kv:
        k_ref, v_ref = rest[:2]
        rest = rest[2:]
    if with_vln:
        vln_ref = rest[0]
        rest = rest[1:]
    s_ref, xq_ref = rest

    tm = x_ref.shape[0]
    xb = x_ref[...].astype(BF16)
    h = _dot(xb, w1_ref[...])
    cos = cos_ref[...]
    sin = sin_ref[...]

    cq = _rms_norm(h[:, P_CQ:P_CKV], qg_ref[...]).astype(BF16)
    qq = _dot(cq, wq_ref[...])
    q_full = qq[:, :QK_SLAB]
    for j in range(ROPE_ALL // LANES):
        qa = qq[:, QK_SLAB + j * LANES:QK_SLAB + (j + 1) * LANES]
        qb = qq[:, QK_SLAB + ROPE_ALL + j * LANES:QK_SLAB + ROPE_ALL + (j + 1) * LANES]
        qr = (qa * cos + qb * sin).astype(BF16)
        q_full = q_full + _dot(qr, eq_ref[j * LANES:(j + 1) * LANES, :])
    q_ref[...] = q_full.astype(BF16)

    ckv = _rms_norm(h[:, P_CKV:P_KRA], kvg_ref[...])
    ckv_ref[...] = ckv
    kr128 = h[:, P_KRA:P_KRB] * cos + h[:, P_KRB:P_U] * sin
    kr_ref[...] = kr128[:, :QK_ROPE_DIM]
    if with_kv:
        ckv_b = ckv.astype(BF16)
        k_full = _dot(ckv_b, wuk_ref[...]) + _dot(kr128.astype(BF16), ek_ref[...])
        k_ref[...] = k_full.astype(BF16)
        v_ref[...] = _dot(ckv_b, wuv_ref[...]).astype(BF16)

    u = jax.nn.gelu(h[:, P_U:P_V])
    vln = _layer_norm(jax.nn.gelu(h[:, P_V:P_XQ]), lng_ref[...], lnb_ref[...])
    if with_vln:
        vln_ref[...] = vln
    vb = vln.astype(BF16)
    row = lax.broadcasted_iota(jnp.int32, (CHUNK, CHUNK), 0)
    col = lax.broadcasted_iota(jnp.int32, (CHUNK, CHUNK), 1)
    mask = col <= row
    if chunk_len < CHUNK:
        mask = mask & ((col // chunk_len) == (row // chunk_len))
    wm = [jnp.where(mask, wmix_ref[g], jnp.zeros((), BF16)) for g in range(N_GROUPS)]
    for c in range(tm // CHUNK):
        rs = slice(c * CHUNK, (c + 1) * CHUNK)
        for g in range(N_GROUPS):
            cs = slice(g * GROUP_DIM, (g + 1) * GROUP_DIM)
            mixed = _dot(wm[g], vb[rs, cs]) + bias_ref[:, cs]
            s_ref[rs, cs] = (u[rs, cs] * mixed).astype(BF16)

    xq_ref[...] = h[:, P_XQ:P_END].astype(BF16)


def _proj_call(x, cos, sin, lw, *, tm, chunk_len, with_kv, with_vln, name):
    rows = x.shape[0]
    grid = (rows // tm,)

    def row_spec(n):
        return pl.BlockSpec((tm, n), lambda i: (i, 0))

    in_specs = [row_spec(D_MODEL), row_spec(LANES), row_spec(LANES),
                _const_spec((D_MODEL, P_END)), _const_spec((1, Q_LORA_RANK)), _const_spec((1, KV_LORA_RANK)),
                _const_spec((Q_LORA_RANK, QK_SLAB + 2 * ROPE_ALL)), _const_spec((ROPE_ALL, QK_SLAB)),
                _const_spec((LANES, QK_SLAB)), _const_spec((KV_LORA_RANK, QK_SLAB)),
                _const_spec((KV_LORA_RANK, N_HEADS * V_HEAD_DIM)), _const_spec((1, SGU_DIM)),
                _const_spec((1, SGU_DIM)), _const_spec((N_GROUPS, CHUNK, CHUNK)), _const_spec((CHUNK, SGU_DIM))]
    out_shape = [jax.ShapeDtypeStruct((rows, KV_LORA_RANK), F32),
                 jax.ShapeDtypeStruct((rows, QK_ROPE_DIM), F32),
                 jax.ShapeDtypeStruct((rows, QK_SLAB), BF16)]
    out_specs = [row_spec(KV_LORA_RANK), row_spec(QK_ROPE_DIM), row_spec(QK_SLAB)]
    if with_kv:
        out_shape += [jax.ShapeDtypeStruct((rows, QK_SLAB), BF16),
                      jax.ShapeDtypeStruct((rows, N_HEADS * V_HEAD_DIM), BF16)]
        out_specs += [row_spec(QK_SLAB), row_spec(N_HEADS * V_HEAD_DIM)]
    if with_vln:
        out_shape += [jax.ShapeDtypeStruct((rows, SGU_DIM), F32)]
        out_specs += [row_spec(SGU_DIM)]
    out_shape += [jax.ShapeDtypeStruct((rows, SGU_DIM), BF16), jax.ShapeDtypeStruct((rows, MEM_DIM), BF16)]
    out_specs += [row_spec(SGU_DIM), row_spec(MEM_DIM)]

    body = functools.partial(_proj_kernel, chunk_len=chunk_len, with_kv=with_kv, with_vln=with_vln)
    return pl.pallas_call(
        body, grid=grid, in_specs=in_specs, out_specs=out_specs, out_shape=out_shape,
        compiler_params=_params(1), name=name,
    )(x, cos, sin, lw["w1"], lw["qg"], lw["kvg"], lw["wq"], lw["eq"], lw["ek"], lw["wuk"], lw["wuv"],
      lw["sgu_g"], lw["sgu_b"], lw["wmix"], lw["mixbias"])


def _memkv_kernel(mem_ref, wk_ref, wv_ref, mk_ref, mv_ref):
    mb = mem_ref[...].astype(BF16)
    mk_ref[...] = _dot(mb, wk_ref[...])
    mv_ref[...] = _dot(mb, wv_ref[...])


def _memkv_call(mem, wk, wv):
    depth = wk.shape[0]
    w_spec = pl.BlockSpec((None, D_MODEL, MEM_DIM), lambda l: (l, 0, 0))
    o_spec = pl.BlockSpec((None, N_MEM, MEM_DIM), lambda l: (l, 0, 0))
    return pl.pallas_call(
        _memkv_kernel, grid=(depth,),
        in_specs=[pl.BlockSpec((N_MEM, D_MODEL), lambda l: (0, 0)), w_spec, w_spec],
        out_specs=[o_spec, o_spec],
        out_shape=[jax.ShapeDtypeStruct((depth, N_MEM, MEM_DIM), F32)] * 2,
        compiler_params=_params(1), name="mem_kv",
    )(mem, wk, wv)


def _memattn_kernel(xq_ref, mk_ref, mv_ref, o_ref, *, bt):
    for b in range(bt):
        xq = xq_ref[b]
        mk = mk_ref[b].astype(BF16)
        mv = mv_ref[b].astype(BF16)
        for hh in range(MEM_HEADS):
            cs = slice(hh * MEM_HEAD_DIM, (hh + 1) * MEM_HEAD_DIM)
            s = _dot_nt(xq[:, cs], mk[:, cs]) * MEM_SCALE
            e = jnp.exp(s - jnp.max(s, axis=-1, keepdims=True))
            p = (e / jnp.sum(e, axis=-1, keepdims=True)).astype(BF16)
            o_ref[b, :, cs] = _dot(p, mv[:, cs]).astype(BF16)


def _memattn_call(xq, mk, mv, *, layer, bt, ts, name):
    nb, s, _ = xq.shape
    kv_spec = pl.BlockSpec((None, bt, N_MEM, MEM_DIM), lambda b, i: (layer, b, 0, 0))
    x_spec = pl.BlockSpec((bt, ts, MEM_DIM), lambda b, i: (b, i, 0))
    return pl.pallas_call(
        functools.partial(_memattn_kernel, bt=bt), grid=(nb // bt, s // ts),
        in_specs=[x_spec, kv_spec, kv_spec], out_specs=x_spec,
        out_shape=jax.ShapeDtypeStruct((nb, s, MEM_DIM), BF16),
        compiler_params=_params(2), name=name,
    )(xq, mk, mv)


def _flash_kernel(q_ref, k_ref, v_ref, o_ref, m_ref, l_ref, acc_ref, *, tq, tk):
    qi = pl.program_id(0)
    ki = pl.program_id(1)

    @pl.when(ki == 0)
    def _():
        m_ref[...] = jnp.full(m_ref.shape, -jnp.inf, F32)
        l_ref[...] = jnp.zeros(l_ref.shape, F32)
        acc_ref[...] = jnp.zeros(acc_ref.shape, F32)

    def step(masked):
        if masked:
            row = lax.broadcasted_iota(jnp.int32, (tq, tk), 0)
            col = lax.broadcasted_iota(jnp.int32, (tq, tk), 1)
            keep = col <= row
        for hh in range(N_HEADS):
            cs = slice(hh * HEAD_SLAB, (hh + 1) * HEAD_SLAB)
            s = _dot_nt(q_ref[:, cs], k_ref[:, cs]) * MLA_SCALE
            if masked:
                s = jnp.where(keep, s, -jnp.inf)
            m_prev = m_ref[hh]
            m_new = jnp.maximum(m_prev, jnp.max(s, axis=-1, keepdims=True))
            alpha = jnp.exp(m_prev - m_new)
            p = jnp.exp(s - m_new)
            l_ref[hh] = alpha * l_ref[hh] + jnp.sum(p, axis=-1, keepdims=True)
            vs = slice((hh // 2) * LANES, (hh // 2 + 1) * LANES)
            acc_ref[hh] = alpha * acc_ref[hh] + _dot(p.astype(BF16), v_ref[:, vs])
            m_ref[hh] = m_new

    @pl.when(ki < qi)
    def _():
        step(False)

    @pl.when(ki == qi)
    def _():
        step(True)
        lane = lax.broadcasted_iota(jnp.int32, (tq, LANES), 1)
        for j in range(N_HEADS // 2):
            even = acc_ref[2 * j] * (1.0 / l_ref[2 * j])
            odd = acc_ref[2 * j + 1] * (1.0 / l_ref[2 * j + 1])
            o_ref[:, j * LANES:(j + 1) * LANES] = jnp.where(lane < V_HEAD_DIM, even, odd).astype(BF16)


def _flash_call(q, k, v, *, tq, name):
    s = q.shape[0]
    tk = tq
    n = s // tq
    return pl.pallas_call(
        functools.partial(_flash_kernel, tq=tq, tk=tk), grid=(n, n),
        in_specs=[pl.BlockSpec((tq, QK_SLAB), lambda qi, ki: (qi, 0)),
                  pl.BlockSpec((tk, QK_SLAB), lambda qi, ki: (jnp.minimum(ki, qi), 0)),
                  pl.BlockSpec((tk, N_HEADS * V_HEAD_DIM), lambda qi, ki: (jnp.minimum(ki, qi), 0))],
        out_specs=pl.BlockSpec((tq, N_HEADS * V_HEAD_DIM), lambda qi, ki: (qi, 0)),
        out_shape=jax.ShapeDtypeStruct((s, N_HEADS * V_HEAD_DIM), BF16),
        scratch_shapes=[pltpu.VMEM((N_HEADS, tq, 1), F32), pltpu.VMEM((N_HEADS, tq, 1), F32),
                        pltpu.VMEM((N_HEADS, tq, LANES), F32)],
        compiler_params=_params(2), name=name,
    )(q, k, v)


def _qcat_kernel(q_ref, w_ref, o_ref):
    for hh in range(N_HEADS):
        o_ref[:, hh, :] = _dot(q_ref[:, hh * HEAD_SLAB:(hh + 1) * HEAD_SLAB], w_ref[hh])


def _qcat_call(q, wcat, *, name):
    rows = q.shape[0]
    return pl.pallas_call(
        _qcat_kernel, grid=(1,),
        in_specs=[pl.BlockSpec((rows, QK_SLAB), lambda i: (0, 0)),
                  pl.BlockSpec((N_HEADS, HEAD_SLAB, QCAT), lambda i: (0, 0, 0))],
        out_specs=pl.BlockSpec((rows, N_HEADS, QCAT), lambda i: (0, 0, 0)),
        out_shape=jax.ShapeDtypeStruct((rows, N_HEADS, QCAT), F32),
        compiler_params=_params(1), name=name,
    )(q, wcat)


def _decode_kernel(pt_ref, q_ref, cnew_ref, krnew_ref, *refs, n_pages_step, n_new):
    del pt_ref
    g = n_pages_step
    c_refs = refs[:g]
    kr_refs = refs[g:2 * g]
    o_ref, m_ref, l_ref, acc_ref = refs[2 * g:]
    j = pl.program_id(1)
    nq = q_ref.shape[1]

    @pl.when(j == 0)
    def _():
        m_ref[...] = jnp.full(m_ref.shape, -jnp.inf, F32)
        l_ref[...] = jnp.zeros(l_ref.shape, F32)
        acc_ref[...] = jnp.zeros(acc_ref.shape, F32)

    q = q_ref[0]
    ql = q[:, :KV_LORA_RANK].astype(BF16)
    qr = q[:, KV_LORA_RANK:KV_LORA_RANK + QK_ROPE_DIM].astype(BF16)
    cs = [c_refs[i][...].astype(BF16) for i in range(g)]
    s_parts = [(_dot_nt(ql, cs[i]) + _dot_nt(qr, kr_refs[i][...].astype(BF16))) * MLA_SCALE for i in range(g)]
    m_prev = m_ref[...]
    m_new = m_prev
    for s in s_parts:
        m_new = jnp.maximum(m_new, jnp.max(s, axis=-1, keepdims=True))
    alpha = jnp.exp(m_prev - m_new)
    l_new = alpha * l_ref[...]
    acc = alpha * acc_ref[...]
    for i in range(g):
        p = jnp.exp(s_parts[i] - m_new)
        l_new = l_new + jnp.sum(p, axis=-1, keepdims=True)
        acc = acc + _dot(p.astype(BF16), cs[i])
    m_ref[...] = m_new
    l_ref[...] = l_new
    acc_ref[...] = acc

    @pl.when(j == pl.num_programs(1) - 1)
    def _():
        qlf = ql.astype(F32)
        qrf = qr.astype(F32)
        tok = lax.broadcasted_iota(jnp.int32, (nq, 1), 0) // N_HEADS
        s_new = []
        cn = []
        for t in range(n_new):
            c_t = cnew_ref[0, t:t + 1, :].astype(BF16).astype(F32)
            kr_t = krnew_ref[0, t:t + 1, :].astype(BF16).astype(F32)
            s_t = (jnp.sum(qlf * c_t, axis=-1, keepdims=True)
                   + jnp.sum(qrf * kr_t, axis=-1, keepdims=True)) * MLA_SCALE
            s_new.append(jnp.where(tok >= t, s_t, -jnp.inf))
            cn.append(c_t)
        m_prev = m_ref[...]
        m_fin = m_prev
        for s_t in s_new:
            m_fin = jnp.maximum(m_fin, s_t)
        alpha = jnp.exp(m_prev - m_fin)
        l_fin = alpha * l_ref[...]
        acc = alpha * acc_ref[...]
        for t in range(n_new):
            p_t = jnp.exp(s_new[t] - m_fin)
            l_fin = l_fin + p_t
            acc = acc + p_t.astype(BF16).astype(F32) * cn[t]
        o_ref[0] = acc * (1.0 / l_fin)


def _decode_call(page_table, qcat, c_new, kr_new, cache_c, cache_kr, *, layer, pages_per_step, name):
    nb, nq, _ = qcat.shape
    n_pages = page_table.shape[1]
    page = cache_c.shape[2]
    g = pages_per_step
    n_new = c_new.shape[1]

    def page_spec(width, i):
        return pl.BlockSpec((None, None, page, width),
                            lambda b, j, pt, i=i: (layer, pt[b, j * g + i], 0, 0))

    in_specs = ([pl.BlockSpec((1, nq, QCAT), lambda b, j, pt: (b, 0, 0)),
                 pl.BlockSpec((1, n_new, KV_LORA_RANK), lambda b, j, pt: (b, 0, 0)),
                 pl.BlockSpec((1, n_new, QK_ROPE_DIM), lambda b, j, pt: (b, 0, 0))]
                + [page_spec(KV_LORA_RANK, i) for i in range(g)]
                + [page_spec(QK_ROPE_DIM, i) for i in range(g)])
    grid_spec = pltpu.PrefetchScalarGridSpec(
        num_scalar_prefetch=1, grid=(nb, n_pages // g), in_specs=in_specs,
        out_specs=pl.BlockSpec((1, nq, KV_LORA_RANK), lambda b, j, pt: (b, 0, 0)),
        scratch_shapes=[pltpu.VMEM((nq, 1), F32), pltpu.VMEM((nq, 1), F32), pltpu.VMEM((nq, KV_LORA_RANK), F32)])
    return pl.pallas_call(
        functools.partial(_decode_kernel, n_pages_step=g, n_new=n_new), grid_spec=grid_spec,
        out_shape=jax.ShapeDtypeStruct((nb, nq, KV_LORA_RANK), F32),
        compiler_params=_params(2), name=name,
    )(page_table, qcat, c_new, kr_new, *([cache_c] * g), *([cache_kr] * g))


def _vup_kernel(o_ref, w_ref, a_ref):
    acc = _dot(o_ref[:, 0, :].astype(BF16), w_ref[0])
    for hh in range(1, N_HEADS):
        acc = acc + _dot(o_ref[:, hh, :].astype(BF16), w_ref[hh])
    a_ref[...] = acc.astype(BF16)


def _vup_call(o_lat, wuv_exp, *, name):
    rows = o_lat.shape[0]
    width = N_HEADS * V_HEAD_DIM
    return pl.pallas_call(
        _vup_kernel, grid=(1,),
        in_specs=[pl.BlockSpec((rows, N_HEADS, KV_LORA_RANK), lambda i: (0, 0, 0)),
                  pl.BlockSpec((N_HEADS, KV_LORA_RANK, width), lambda i: (0, 0, 0))],
        out_specs=pl.BlockSpec((rows, width), lambda i: (0, 0)),
        out_shape=jax.ShapeDtypeStruct((rows, width), BF16),
        compiler_params=_params(1), name=name,
    )(o_lat, wuv_exp)


def _merge_kernel(x_ref, a_ref, s_ref, m_ref, wg_ref, bg_ref, wbr_ref, wo_ref, g_ref, b_ref, o_ref, *, alpha):
    xf = x_ref[...]
    xb = xf.astype(BF16)
    merged = None
    for i, br in enumerate((a_ref, s_ref, m_ref)):
        gate = jax.nn.sigmoid(_dot(xb, wg_ref[:, i * D_MODEL:(i + 1) * D_MODEL]) + bg_ref[i:i + 1, :])
        term = gate * _dot(br[...], wbr_ref[i])
        merged = term if merged is None else merged + term
    y = alpha * xf + _dot(merged.astype(BF16), wo_ref[...])
    o_ref[...] = _layer_norm(y, g_ref[...], b_ref[...])


def _merge_call(x, a, s, m, lw, *, tm, alpha, name):
    rows = x.shape[0]

    def row_spec(n):
        return pl.BlockSpec((tm, n), lambda i: (i, 0))

    return pl.pallas_call(
        functools.partial(_merge_kernel, alpha=alpha), grid=(rows // tm,),
        in_specs=[row_spec(D_MODEL), row_spec(BRANCH_DIM), row_spec(BRANCH_DIM), row_spec(BRANCH_DIM),
                  _const_spec((D_MODEL, N_BRANCH * D_MODEL)), _const_spec((N_BRANCH, D_MODEL)),
                  _const_spec((N_BRANCH, BRANCH_DIM, D_MODEL)), _const_spec((D_MODEL, D_MODEL)),
                  _const_spec((1, D_MODEL)), _const_spec((1, D_MODEL))],
        out_specs=row_spec(D_MODEL),
        out_shape=jax.ShapeDtypeStruct((rows, D_MODEL), F32),
        compiler_params=_params(1), name=name,
    )(x, a, s, m, lw["wg"], lw["bg"], lw["wbr"], lw["wo"], lw["ln1_g"], lw["ln1_b"])


def _ffn_kernel(x_ref, w1_ref, w3_ref, w2_ref, g_ref, b_ref, o_ref, *, alpha, n_split):
    xf = x_ref[...]
    xb = xf.astype(BF16)
    d_ff = w1_ref.shape[1]
    step = d_ff // n_split
    ff = None
    for c in range(n_split):
        cs = slice(c * step, (c + 1) * step)
        act = (jax.nn.silu(_dot(xb, w1_ref[:, cs])) * _dot(xb, w3_ref[:, cs])).astype(BF16)
        part = _dot(act, w2_ref[cs, :])
        ff = part if ff is None else ff + part
    o_ref[...] = _layer_norm(alpha * xf + ff, g_ref[...], b_ref[...])


def _ffn_call(x, lw, *, tm, alpha, name):
    rows = x.shape[0]
    d_ff = lw["w_ff1"].shape[1]
    row_spec = pl.BlockSpec((tm, D_MODEL), lambda i: (i, 0))
    return pl.pallas_call(
        functools.partial(_ffn_kernel, alpha=alpha, n_split=2), grid=(rows // tm,),
        in_specs=[row_spec, _const_spec((D_MODEL, d_ff)), _const_spec((D_MODEL, d_ff)),
                  _const_spec((d_ff, D_MODEL)), _const_spec((1, D_MODEL)), _const_spec((1, D_MODEL))],
        out_specs=row_spec,
        out_shape=jax.ShapeDtypeStruct((rows, D_MODEL), F32),
        compiler_params=_params(1), name=name,
    )(x, lw["w_ff1"], lw["w_ff3"], lw["w_ff2"], lw["ln2_g"], lw["ln2_b"])


def _rot_cols(w):
    half = w.shape[-1] // 2
    return jnp.concatenate([-w[..., half:], w[..., :half]], axis=-1)


def _placement(rows, cols, pairs):
    e = np.zeros((rows, cols), np.float32)
    for r, c in pairs:
        e[r, c] = 1.0
    return jnp.asarray(e, dtype=BF16)


def _prep_layer(l, chunk_lens, w_in, b_gate, q_norm_g, kv_norm_g, w_uq, w_uk, w_uv, sgu_ln_g, sgu_ln_b, w_s, b_s,
                w_br, w_o, ln1_g, ln1_b, w_ff1, w_ff3, w_ff2, ln2_g, ln2_b):
    wi = w_in[l]
    wkr = wi[:, OFF_KR:OFF_U]
    zpad = jnp.zeros((D_MODEL, LANES - QK_ROPE_DIM), F32)
    w1 = jnp.concatenate([wi[:, OFF_CQ:OFF_KR], wkr, zpad, _rot_cols(wkr), zpad, wi[:, OFF_U:OFF_G]], axis=1)

    wuq = w_uq[l].reshape(Q_LORA_RANK, N_HEADS, QK_HEAD_DIM)
    wq_nope = jnp.pad(wuq[:, :, :QK_NOPE_DIM], ((0, 0), (0, 0), (0, HEAD_SLAB - QK_NOPE_DIM)))
    wq_rope = wuq[:, :, QK_NOPE_DIM:]
    wq = jnp.concatenate([wq_nope.reshape(Q_LORA_RANK, QK_SLAB), wq_rope.reshape(Q_LORA_RANK, ROPE_ALL),
                          _rot_cols(wq_rope).reshape(Q_LORA_RANK, ROPE_ALL)], axis=1)
    eq = _placement(ROPE_ALL, QK_SLAB, [(hh * QK_ROPE_DIM + d, hh * HEAD_SLAB + QK_NOPE_DIM + d)
                                        for hh in range(N_HEADS) for d in range(QK_ROPE_DIM)])
    ek = _placement(LANES, QK_SLAB, [(d, hh * HEAD_SLAB + QK_NOPE_DIM + d)
                                     for hh in range(N_HEADS) for d in range(QK_ROPE_DIM)])
    wuk = jnp.pad(w_uk[l], ((0, 0), (0, 0), (0, HEAD_SLAB - QK_NOPE_DIM))).reshape(KV_LORA_RANK, QK_SLAB)
    wuv = w_uv[l].reshape(KV_LORA_RANK, N_HEADS * V_HEAD_DIM)

    wcat = jnp.zeros((N_HEADS, HEAD_SLAB, QCAT), F32)
    wcat = wcat.at[:, :QK_NOPE_DIM, :KV_LORA_RANK].set(jnp.transpose(w_uk[l], (1, 2, 0)))
    d = np.arange(QK_ROPE_DIM)
    wcat = wcat.at[:, QK_NOPE_DIM + d, KV_LORA_RANK + d].set(1.0)
    wuv_exp = jnp.zeros((N_HEADS, KV_LORA_RANK, N_HEADS * V_HEAD_DIM), F32)
    for hh in range(N_HEADS):
        wuv_exp = wuv_exp.at[hh, :, hh * V_HEAD_DIM:(hh + 1) * V_HEAD_DIM].set(w_uv[l][:, hh, :])

    lw = {
        "w1": w1.astype(BF16), "qg": q_norm_g[l][None, :], "kvg": kv_norm_g[l][None, :],
        "wq": wq.astype(BF16), "eq": eq, "ek": ek, "wuk": wuk.astype(BF16), "wuv": wuv.astype(BF16),
        "sgu_g": sgu_ln_g[l][None, :], "sgu_b": sgu_ln_b[l][None, :],
        "wcat": wcat.astype(BF16), "wuv_exp": wuv_exp.astype(BF16),
        "wg": wi[:, OFF_G:].astype(BF16), "bg": b_gate[l], "wbr": w_br[l].astype(BF16), "wo": w_o[l].astype(BF16),
        "ln1_g": ln1_g[l][None, :], "ln1_b": ln1_b[l][None, :],
        "w_ff1": w_ff1[l].astype(BF16), "w_ff3": w_ff3[l].astype(BF16), "w_ff2": w_ff2[l].astype(BF16),
        "ln2_g": ln2_g[l][None, :], "ln2_b": ln2_b[l][None, :],
    }
    for key, cl in chunk_lens.items():
        reps = CHUNK // cl
        lw["wmix_" + key] = jnp.tile(w_s[l][:, :cl, :cl], (1, reps, reps)).astype(BF16)
        bias = jnp.tile(jnp.transpose(b_s[l][:, :cl]), (reps, 1))
        lw["mixbias_" + key] = jnp.repeat(bias, GROUP_DIM, axis=1)
    return lw


def _rope_tables(pos):
    half = QK_ROPE_DIM // 2
    inv_freq = jnp.power(ROPE_THETA, -jnp.arange(half, dtype=F32) / half)
    ang = pos.astype(F32)[:, None] * inv_freq[None, :]
    reps = LANES // half
    return jnp.tile(jnp.cos(ang), (1, reps)), jnp.tile(jnp.sin(ang), (1, reps))


def kernel(x_prompt, x_sample, mem_prompt, cache_kv_latent, cache_k_rope, cache_mem_k, cache_mem_v, page_table,
           w_in, b_gate, q_norm_g, kv_norm_g, w_uq, w_uk, w_uv, sgu_ln_g, sgu_ln_b, w_s, b_s, w_mk, w_mv,
           w_br, w_o, ln1_g, ln1_b, w_ff1, w_ff3, w_ff2, ln2_g, ln2_b):
    nbp, seq, _ = x_prompt.shape
    nbs, tdec, _ = x_sample.shape
    depth = w_in.shape[0]
    n_pages = page_table.shape[1]
    past = n_pages * cache_kv_latent.shape[2]
    alpha = (2 * depth) ** 0.25
    assert nbp == 1 and seq % CHUNK == 0 and tdec < CHUNK and CHUNK % tdec == 0

    cos_p, sin_p = _rope_tables(jnp.arange(seq, dtype=jnp.int32))
    cos_s, sin_s = _rope_tables(jnp.tile(past + jnp.arange(tdec, dtype=jnp.int32), nbs))
    chunk_lens = {"p": min(seq, CHUNK), "s": min(tdec, CHUNK)}

    mk_p_all, mv_p_all = _memkv_call(mem_prompt[0], w_mk.astype(BF16), w_mv.astype(BF16))
    mk_p4 = mk_p_all[:, None]
    mv_p4 = mv_p_all[:, None]
    cache_mk = cache_mem_k.reshape(depth, nbs, N_MEM, MEM_DIM)
    cache_mv = cache_mem_v.reshape(depth, nbs, N_MEM, MEM_DIM)

    xp = x_prompt.reshape(seq, D_MODEL)
    xs = x_sample.reshape(nbs * tdec, D_MODEL)
    ckv_p, kr_p, ckv_s, kr_s, v_s = [], [], [], [], []
    for l in range(depth):
        lw = _prep_layer(l, chunk_lens, w_in, b_gate, q_norm_g, kv_norm_g, w_uq, w_uk, w_uv, sgu_ln_g, sgu_ln_b,
                         w_s, b_s, w_br, w_o, ln1_g, ln1_b, w_ff1, w_ff3, w_ff2, ln2_g, ln2_b)

        lw_p = dict(lw, wmix=lw["wmix_p"], mixbias=lw["mixbias_p"])
        c_new, k_new, q, k, v, s_out, xq = _proj_call(
            xp, cos_p, sin_p, lw_p, tm=512, chunk_len=chunk_lens["p"], with_kv=True, with_vln=False,
            name=f"proj_prompt_{l}")
        ckv_p.append(c_new)
        kr_p.append(k_new)
        a_out = _flash_call(q, k, v, tq=512, name=f"flash_prompt_{l}")
        m_out = _memattn_call(xq[None], mk_p4, mv_p4, layer=l, bt=1, ts=512, name=f"memattn_prompt_{l}")[0]
        x1 = _merge_call(xp, a_out, s_out, m_out, lw, tm=512, alpha=alpha, name=f"merge_prompt_{l}")
        xp = _ffn_call(x1, lw, tm=512, alpha=alpha, name=f"ffn_prompt_{l}")

        lw_s = dict(lw, wmix=lw["wmix_s"], mixbias=lw["mixbias_s"])
        c_new, k_new, q, vln, s_out, xq = _proj_call(
            xs, cos_s, sin_s, lw_s, tm=256, chunk_len=chunk_lens["s"], with_kv=False, with_vln=True,
            name=f"proj_sample_{l}")
        ckv_s.append(c_new)
        kr_s.append(k_new)
        v_s.append(vln)
        qcat = _qcat_call(q, lw["wcat"], name=f"qcat_sample_{l}")
        o_lat = _decode_call(page_table, qcat.reshape(nbs, tdec * N_HEADS, QCAT),
                             c_new.reshape(nbs, tdec, KV_LORA_RANK), k_new.reshape(nbs, tdec, QK_ROPE_DIM),
                             cache_kv_latent, cache_k_rope, layer=l, pages_per_step=16, name=f"decode_sample_{l}")
        a_out = _vup_call(o_lat.reshape(nbs * tdec, N_HEADS, KV_LORA_RANK), lw["wuv_exp"], name=f"vup_sample_{l}")
        m_out = _memattn_call(xq.reshape(nbs, tdec, MEM_DIM), cache_mk, cache_mv, layer=l, bt=8, ts=tdec,
                              name=f"memattn_sample_{l}").reshape(nbs * tdec, MEM_DIM)
        x1 = _merge_call(xs, a_out, s_out, m_out, lw, tm=256, alpha=alpha, name=f"merge_sample_{l}")
        xs = _ffn_call(x1, lw, tm=256, alpha=alpha, name=f"ffn_sample_{l}")

    return (xp.reshape(nbp, seq, D_MODEL),
            xs.reshape(nbs, tdec, D_MODEL),
            jnp.stack(ckv_p).reshape(depth, nbp, seq, KV_LORA_RANK),
            jnp.stack(kr_p).reshape(depth, nbp, seq, QK_ROPE_DIM),
            mk_p_all.reshape(depth, nbp, N_MEM, MEM_HEADS, MEM_HEAD_DIM),
            mv_p_all.reshape(depth, nbp, N_MEM, MEM_HEADS, MEM_HEAD_DIM),
            jnp.stack(ckv_s).reshape(depth, nbs, tdec, KV_LORA_RANK),
            jnp.stack(kr_s).reshape(depth, nbs, tdec, QK_ROPE_DIM),
            jnp.stack(v_s).reshape(depth, nbs, tdec, SGU_DIM))
```

```python
import functools
import math

import numpy as np
import jax
import jax.numpy as jnp
from jax import lax
from jax.experimental import pallas as pl
from jax.experimental.pallas import tpu as pltpu

F32 = jnp.float32
BF16 = jnp.bfloat16

D_MODEL = 1024
N_HEADS = 8
QK_NOPE_DIM = 64
QK_ROPE_DIM = 32
QK_HEAD_DIM = QK_NOPE_DIM + QK_ROPE_DIM
V_HEAD_DIM = 64
Q_LORA_RANK = 384
KV_LORA_RANK = 256
ROPE_THETA = 10000.0
MLA_SCALE = QK_HEAD_DIM ** -0.5
CHUNK = 128
N_GROUPS = 4
GROUP_DIM = 128
SGU_DIM = N_GROUPS * GROUP_DIM
N_MEM = 256
MEM_HEADS = 4
MEM_HEAD_DIM = 128
MEM_DIM = MEM_HEADS * MEM_HEAD_DIM
MEM_SCALE = MEM_HEAD_DIM ** -0.5
N_BRANCH = 3
BRANCH_DIM = 512
LN_EPS = 1e-5
RMS_EPS = 1e-6
OFF_CQ = 0
OFF_CKV = OFF_CQ + Q_LORA_RANK
OFF_KR = OFF_CKV + KV_LORA_RANK
OFF_U = OFF_KR + QK_ROPE_DIM
OFF_V = OFF_U + SGU_DIM
OFF_XQ = OFF_V + SGU_DIM
OFF_G = OFF_XQ + MEM_DIM

LANES = 128
HEAD_SLAB = LANES
QK_SLAB = N_HEADS * HEAD_SLAB
ROPE_ALL = N_HEADS * QK_ROPE_DIM
P_CQ = 0
P_CKV = P_CQ + Q_LORA_RANK
P_KRA = P_CKV + KV_LORA_RANK
P_KRB = P_KRA + LANES
P_U = P_KRB + LANES
P_V = P_U + SGU_DIM
P_XQ = P_V + SGU_DIM
P_END = P_XQ + MEM_DIM
QCAT = 384
SUBLANES = 8
VT_ROWS = 80
VT_ALL = N_HEADS * VT_ROWS
EXP2_SCALE = MLA_SCALE * math.log2(math.e)
VMEM_LIMIT = 56 * 1024 * 1024


def _dot(a, b):
    return jnp.dot(a, b, preferred_element_type=F32)


def _dot_nt(a, b):
    return lax.dot_general(a, b, (((1,), (1,)), ((), ())), preferred_element_type=F32)


def _const_spec(shape):
    nd = len(shape)
    return pl.BlockSpec(shape, lambda *_: (0,) * nd, pipeline_mode=pl.Buffered(1))


def _params(n_axes):
    return pltpu.CompilerParams(dimension_semantics=("arbitrary",) * n_axes,
                                vmem_limit_bytes=VMEM_LIMIT)


def _layer_norm(x, g, b):
    mu = jnp.mean(x, axis=-1, keepdims=True)
    xc = x - mu
    var = jnp.mean(xc * xc, axis=-1, keepdims=True)
    return xc * lax.rsqrt(var + LN_EPS) * g + b


def _rms_norm(x, g):
    ms = jnp.mean(x * x, axis=-1, keepdims=True)
    return x * lax.rsqrt(ms + RMS_EPS) * g


def _proj_kernel(x_ref, cos_ref, sin_ref, w1_ref, qg_ref, kvg_ref, wq_ref, eq_ref, ek_ref, wuk_ref,
                 wuvt_ref, vones_ref, lng_ref, lnb_ref, wmix_ref, bias_ref, *out_refs, chunk_len, with_kv, with_vln):
    out_refs = list(out_refs)
    ckv_ref, kr_ref, q_ref = out_refs[:3]
    rest = out_refs[3:]
    if with_kv:
        k_ref, vt_ref = rest[:2]
        rest = rest[2:]
    if with_vln:
        vln_ref = rest[0]
        rest = rest[1:]
    s_ref, xq_ref = rest

    tm = x_ref.shape[0]
    xb = x_ref[...].astype(BF16)
    h = _dot(xb, w1_ref[...])
    cos = cos_ref[...]
    sin = sin_ref[...]

    cq = _rms_norm(h[:, P_CQ:P_CKV], qg_ref[...]).astype(BF16)
    qq = _dot(cq, wq_ref[...])
    q_full = qq[:, :QK_SLAB]
    for j in range(ROPE_ALL // LANES):
        qa = qq[:, QK_SLAB + j * LANES:QK_SLAB + (j + 1) * LANES]
        qb = qq[:, QK_SLAB + ROPE_ALL + j * LANES:QK_SLAB + ROPE_ALL + (j + 1) * LANES]
        qr = (qa * cos + qb * sin).astype(BF16)
        q_full = q_full + _dot(qr, eq_ref[j * LANES:(j + 1) * LANES, :])
    q_ref[...] = q_full.astype(BF16)

    ckv = _rms_norm(h[:, P_CKV:P_KRA], kvg_ref[...])
    ckv_ref[...] = ckv
    kr128 = h[:, P_KRA:P_KRB] * cos + h[:, P_KRB:P_U] * sin
    kr_ref[...] = kr128[:, :QK_ROPE_DIM]
    if with_kv:
        ckv_b = ckv.astype(BF16)
        k_full = _dot(ckv_b, wuk_ref[...]) + _dot(kr128.astype(BF16), ek_ref[...])
        k_ref[...] = k_full.astype(BF16)
        vt_ref[...] = (_dot_nt(wuvt_ref[...], ckv_b) + vones_ref[...]).astype(BF16)

    u = jax.nn.gelu(h[:, P_U:P_V])
    vln = _layer_norm(jax.nn.gelu(h[:, P_V:P_XQ]), lng_ref[...], lnb_ref[...])
    if with_vln:
        vln_ref[...] = vln
    vb = vln.astype(BF16)
    row = lax.broadcasted_iota(jnp.int32, (CHUNK, CHUNK), 0)
    col = lax.broadcasted_iota(jnp.int32, (CHUNK, CHUNK), 1)
    mask = col <= row
    if chunk_len < CHUNK:
        mask = mask & ((col // chunk_len) == (row // chunk_len))
    wm = [jnp.where(mask, wmix_ref[g], jnp.zeros((), BF16)) for g in range(N_GROUPS)]
    for c in range(tm // CHUNK):
        rs = slice(c * CHUNK, (c + 1) * CHUNK)
        for g in range(N_GROUPS):
            cs = slice(g * GROUP_DIM, (g + 1) * GROUP_DIM)
            mixed = _dot(wm[g], vb[rs, cs]) + bias_ref[:, cs]
            s_ref[rs, cs] = (u[rs, cs] * mixed).astype(BF16)

    xq_ref[...] = h[:, P_XQ:P_END].astype(BF16)


def _proj_call(x, cos, sin, lw, *, tm, chunk_len, with_kv, with_vln, name):
    rows = x.shape[0]
    grid = (rows // tm,)

    def row_spec(n):
        return pl.BlockSpec((tm, n), lambda i: (i, 0))

    in_specs = [row_spec(D_MODEL), row_spec(LANES), row_spec(LANES),
                _const_spec((D_MODEL, P_END)), _const_spec((1, Q_LORA_RANK)), _const_spec((1, KV_LORA_RANK)),
                _const_spec((Q_LORA_RANK, QK_SLAB + 2 * ROPE_ALL)), _const_spec((ROPE_ALL, QK_SLAB)),
                _const_spec((LANES, QK_SLAB)), _const_spec((KV_LORA_RANK, QK_SLAB)),
                _const_spec((VT_ALL, KV_LORA_RANK)), _const_spec((VT_ALL, 1)), _const_spec((1, SGU_DIM)),
                _const_spec((1, SGU_DIM)), _const_spec((N_GROUPS, CHUNK, CHUNK)), _const_spec((CHUNK, SGU_DIM))]
    out_shape = [jax.ShapeDtypeStruct((rows, KV_LORA_RANK), F32),
                 jax.ShapeDtypeStruct((rows, QK_ROPE_DIM), F32),
                 jax.ShapeDtypeStruct((rows, QK_SLAB), BF16)]
    out_specs = [row_spec(KV_LORA_RANK), row_spec(QK_ROPE_DIM), row_spec(QK_SLAB)]
    if with_kv:
        out_shape += [jax.ShapeDtypeStruct((rows, QK_SLAB), BF16),
                      jax.ShapeDtypeStruct((VT_ALL, rows), BF16)]
        out_specs += [row_spec(QK_SLAB), pl.BlockSpec((VT_ALL, tm), lambda i: (0, i))]
    if with_vln:
        out_shape += [jax.ShapeDtypeStruct((rows, SGU_DIM), F32)]
        out_specs += [row_spec(SGU_DIM)]
    out_shape += [jax.ShapeDtypeStruct((rows, SGU_DIM), BF16), jax.ShapeDtypeStruct((rows, MEM_DIM), BF16)]
    out_specs += [row_spec(SGU_DIM), row_spec(MEM_DIM)]

    body = functools.partial(_proj_kernel, chunk_len=chunk_len, with_kv=with_kv, with_vln=with_vln)
    return pl.pallas_call(
        body, grid=grid, in_specs=in_specs, out_specs=out_specs, out_shape=out_shape,
        compiler_params=_params(1), name=name,
    )(x, cos, sin, lw["w1"], lw["qg"], lw["kvg"], lw["wq"], lw["eq"], lw["ek"], lw["wuk"], lw["wuvt"],
      lw["vones"], lw["sgu_g"], lw["sgu_b"], lw["wmix"], lw["mixbias"])


def _memkv_kernel(mem_ref, wk_ref, wv_ref, mk_ref, mv_ref):
    mb = mem_ref[...].astype(BF16)
    mk_ref[...] = _dot(mb, wk_ref[...])
    mv_ref[...] = _dot(mb, wv_ref[...])


def _memkv_call(mem, wk, wv):
    depth = wk.shape[0]
    w_spec = pl.BlockSpec((None, D_MODEL, MEM_DIM), lambda l: (l, 0, 0))
    o_spec = pl.BlockSpec((None, N_MEM, MEM_DIM), lambda l: (l, 0, 0))
    return pl.pallas_call(
        _memkv_kernel, grid=(depth,),
        in_specs=[pl.BlockSpec((N_MEM, D_MODEL), lambda l: (0, 0)), w_spec, w_spec],
        out_specs=[o_spec, o_spec],
        out_shape=[jax.ShapeDtypeStruct((depth, N_MEM, MEM_DIM), F32)] * 2,
        compiler_params=_params(1), name="mem_kv",
    )(mem, wk, wv)


def _memattn_kernel(xq_ref, mk_ref, mv_ref, o_ref, *, bt):
    pairs = [(b, hh) for b in range(bt) for hh in range(MEM_HEADS)]

    def head_rows(ref, b, hh):
        return ref[b, pl.ds(hh, N_MEM, stride=MEM_HEADS), :].astype(BF16)

    def cols(hh):
        return slice(hh * MEM_HEAD_DIM, (hh + 1) * MEM_HEAD_DIM)

    scores = [_dot_nt(xq_ref[b, :, cols(hh)], head_rows(mk_ref, b, hh)) * MEM_SCALE for b, hh in pairs]
    probs = []
    for s in scores:
        e = jnp.exp(s - jnp.max(s, axis=-1, keepdims=True))
        probs.append((e / jnp.sum(e, axis=-1, keepdims=True)).astype(BF16))
    for (b, hh), p in zip(pairs, probs):
        o_ref[b, :, cols(hh)] = _dot(p, head_rows(mv_ref, b, hh)).astype(BF16)


def _memattn_call(xq, mk, mv, *, layer, bt, ts, name):
    nb, s, _ = xq.shape
    kv_spec = pl.BlockSpec((None, bt, N_MEM * MEM_HEADS, MEM_HEAD_DIM), lambda b, i: (layer, b, 0, 0))
    x_spec = pl.BlockSpec((bt, ts, MEM_DIM), lambda b, i: (b, i, 0))
    return pl.pallas_call(
        functools.partial(_memattn_kernel, bt=bt), grid=(nb // bt, s // ts),
        in_specs=[x_spec, kv_spec, kv_spec], out_specs=x_spec,
        out_shape=jax.ShapeDtypeStruct((nb, s, MEM_DIM), BF16),
        compiler_params=_params(2), name=name,
    )(xq, mk, mv)


def _flash_kernel(qtab_ref, ktab_ref, q_ref, k_ref, vt_ref, o_ref, m_ref, acc_ref, *, tq, tk):
    t = pl.program_id(0)
    qi = qtab_ref[t]
    ki = ktab_ref[t]

    @pl.when(ki == 0)
    def _():
        m_ref[...] = jnp.full(m_ref.shape, -jnp.inf, F32)
        acc_ref[...] = jnp.zeros(acc_ref.shape, F32)

    def scores(hh):
        cs = slice(hh * HEAD_SLAB, (hh + 1) * HEAD_SLAB)
        return _dot_nt(k_ref[:, cs], q_ref[:, cs])

    def update(hh, st, keep):
        if keep is not None:
            st = jnp.where(keep, st, -jnp.inf)
        s3 = st.reshape(tk // SUBLANES, SUBLANES, tq)
        m_prev = m_ref[hh]
        m_cur = jnp.max(jnp.max(s3, axis=0), axis=0, keepdims=True)
        m_new = jnp.maximum(m_prev, m_cur)
        alpha = jnp.exp2((m_prev - m_new) * EXP2_SCALE)
        p = jnp.exp2((s3 - m_new[None]) * EXP2_SCALE).reshape(tk, tq).astype(BF16)
        rs = slice(hh * VT_ROWS, (hh + 1) * VT_ROWS)
        acc_ref[rs, :] = alpha[0:1] * acc_ref[rs, :] + _dot(vt_ref[rs, :], p)
        m_ref[hh] = m_new

    def step(masked):
        keep = None
        if masked:
            key = lax.broadcasted_iota(jnp.int32, (tk, tq), 0) + ki * tk
            qry = lax.broadcasted_iota(jnp.int32, (tk, tq), 1) + qi * tq
            keep = key <= qry
        st = scores(0)
        for hh in range(N_HEADS):
            nxt = scores(hh + 1) if hh + 1 < N_HEADS else None
            update(hh, st, keep)
            st = nxt

    last = (qi * tq + tq - 1) // tk
    full = (ki + 1) * tk - 1 <= qi * tq

    @pl.when(full)
    def _():
        step(False)

    @pl.when(jnp.logical_not(full))
    def _():
        step(True)

    @pl.when(ki == last)
    def _():
        outs = []
        for hh in range(N_HEADS):
            base = hh * VT_ROWS
            inv = 1.0 / acc_ref[base + V_HEAD_DIM:base + V_HEAD_DIM + 1, :]
            outs.append(acc_ref[base:base + V_HEAD_DIM, :] * inv)
        o_ref[...] = jnp.concatenate(outs, axis=0).T.astype(BF16)


def _flash_call(q, k, vt, *, tq, tk, name):
    s = q.shape[0]
    steps = [(qi, ki) for qi in range(s // tq) for ki in range((qi * tq + tq - 1) // tk + 1)]
    qtab = jnp.asarray(np.array([p[0] for p in steps], np.int32))
    ktab = jnp.asarray(np.array([p[1] for p in steps], np.int32))
    width = N_HEADS * V_HEAD_DIM
    grid_spec = pltpu.PrefetchScalarGridSpec(
        num_scalar_prefetch=2, grid=(len(steps),),
        in_specs=[pl.BlockSpec((tq, QK_SLAB), lambda t, qt, kt: (qt[t], 0)),
                  pl.BlockSpec((tk, QK_SLAB), lambda t, qt, kt: (kt[t], 0)),
                  pl.BlockSpec((VT_ALL, tk), lambda t, qt, kt: (0, kt[t]))],
        out_specs=pl.BlockSpec((tq, width), lambda t, qt, kt: (qt[t], 0)),
        scratch_shapes=[pltpu.VMEM((N_HEADS, SUBLANES, tq), F32), pltpu.VMEM((VT_ALL, tq), F32)])
    return pl.pallas_call(
        functools.partial(_flash_kernel, tq=tq, tk=tk), grid_spec=grid_spec,
        out_shape=jax.ShapeDtypeStruct((s, width), BF16),
        compiler_params=_params(1), name=name,
    )(qtab, ktab, q, k, vt)


def _qcat_kernel(q_ref, w_ref, o_ref):
    for hh in range(N_HEADS):
        o_ref[:, hh, :] = _dot(q_ref[:, hh * HEAD_SLAB:(hh + 1) * HEAD_SLAB], w_ref[hh])


def _qcat_call(q, wcat, *, name):
    rows = q.shape[0]
    return pl.pallas_call(
        _qcat_kernel, grid=(1,),
        in_specs=[pl.BlockSpec((rows, QK_SLAB), lambda i: (0, 0)),
                  pl.BlockSpec((N_HEADS, HEAD_SLAB, QCAT), lambda i: (0, 0, 0))],
        out_specs=pl.BlockSpec((rows, N_HEADS, QCAT), lambda i: (0, 0, 0)),
        out_shape=jax.ShapeDtypeStruct((rows, N_HEADS, QCAT), F32),
        compiler_params=_params(1), name=name,
    )(q, wcat)


def _decode_kernel(pt_ref, q_ref, cnew_ref, krnew_ref, cache_c_ref, cache_krt_ref, o_ref,
                   cbuf, krbuf, csem, krsem, m_ref, l_ref, acc_ref, *, layer, n_pages_step, steps_per_batch, n_split, n_new):
    g = n_pages_step
    page = cbuf.shape[1] // g
    t = pl.program_id(0)
    j = t % steps_per_batch
    slot = t % 2
    nq = q_ref.shape[1]

    def page_copies(step, buf):
        b = step // steps_per_batch
        first = (step % steps_per_batch) * g
        copies = []
        for i in range(g):
            pg = pt_ref[b, first + i]
            copies.append(pltpu.make_async_copy(
                cache_c_ref.at[layer, pg], cbuf.at[buf, pl.ds(i * page, page), :], csem.at[buf]))
            copies.append(pltpu.make_async_copy(
                cache_krt_ref.at[layer, pg], krbuf.at[buf, :, pl.ds(i * page, page)], krsem.at[buf]))
        return copies

    @pl.when(t == 0)
    def _():
        for cp in page_copies(t, slot):
            cp.start()

    @pl.when(t + 1 < pl.num_programs(0))
    def _():
        for cp in page_copies(t + 1, 1 - slot):
            cp.start()

    @pl.when(j == 0)
    def _():
        m_ref[...] = jnp.full(m_ref.shape, -jnp.inf, F32)
        l_ref[...] = jnp.zeros(l_ref.shape, F32)
        acc_ref[...] = jnp.zeros(acc_ref.shape, F32)

    q = q_ref[0]
    ql = q[:, :KV_LORA_RANK].astype(BF16)
    qr = q[:, KV_LORA_RANK:KV_LORA_RANK + QK_ROPE_DIM].astype(BF16)

    for cp in page_copies(t, slot):
        cp.wait()

    keys = (g * page) // n_split

    def chunk_scores(c):
        kb = cbuf[slot, pl.ds(c * keys, keys), :].astype(BF16)
        krt = krbuf[slot, :, pl.ds(c * keys, keys)].astype(BF16)
        return kb, (_dot_nt(ql, kb) + _dot(qr, krt)) * MLA_SCALE

    m_run = m_ref[...]
    l_run = l_ref[...]
    acc = acc_ref[...]
    kb, s = chunk_scores(0)
    for c in range(n_split):
        nxt = chunk_scores(c + 1) if c + 1 < n_split else None
        m_new = jnp.maximum(m_run, jnp.max(s, axis=-1, keepdims=True))
        alpha = jnp.exp(m_run - m_new)
        p = jnp.exp(s - m_new)
        l_run = alpha * l_run + jnp.sum(p, axis=-1, keepdims=True)
        acc = alpha * acc + _dot(p.astype(BF16), kb)
        m_run = m_new
        if nxt is not None:
            kb, s = nxt
    m_ref[...] = m_run
    l_ref[...] = l_run
    acc_ref[...] = acc

    @pl.when(j == steps_per_batch - 1)
    def _():
        qlf = ql.astype(F32)
        qrf = qr.astype(F32)
        tok = lax.broadcasted_iota(jnp.int32, (nq, 1), 0) // N_HEADS
        s_new = []
        cn = []
        for t in range(n_new):
            c_t = cnew_ref[0, t:t + 1, :].astype(BF16).astype(F32)
            kr_t = krnew_ref[0, t:t + 1, :].astype(BF16).astype(F32)
            s_t = (jnp.sum(qlf * c_t, axis=-1, keepdims=True)
                   + jnp.sum(qrf * kr_t, axis=-1, keepdims=True)) * MLA_SCALE
            s_new.append(jnp.where(tok >= t, s_t, -jnp.inf))
            cn.append(c_t)
        m_prev = m_ref[...]
        m_fin = m_prev
        for s_t in s_new:
            m_fin = jnp.maximum(m_fin, s_t)
        alpha = jnp.exp(m_prev - m_fin)
        l_fin = alpha * l_ref[...]
        acc = alpha * acc_ref[...]
        for t in range(n_new):
            p_t = jnp.exp(s_new[t] - m_fin)
            l_fin = l_fin + p_t
            acc = acc + p_t.astype(BF16).astype(F32) * cn[t]
        o_ref[0] = acc * (1.0 / l_fin)


def _decode_call(page_table, qcat, c_new, kr_new, cache_c, cache_krt, *, layer, pages_per_step, n_split, name):
    nb, nq, _ = qcat.shape
    n_pages = page_table.shape[1]
    page = cache_c.shape[2]
    g = pages_per_step
    spb = n_pages // g
    n_new = c_new.shape[1]
    in_specs = [pl.BlockSpec((1, nq, QCAT), lambda t, pt: (t // spb, 0, 0)),
                pl.BlockSpec((1, n_new, KV_LORA_RANK), lambda t, pt: (t // spb, 0, 0)),
                pl.BlockSpec((1, n_new, QK_ROPE_DIM), lambda t, pt: (t // spb, 0, 0)),
                pl.BlockSpec(memory_space=pl.ANY), pl.BlockSpec(memory_space=pl.ANY)]
    grid_spec = pltpu.PrefetchScalarGridSpec(
        num_scalar_prefetch=1, grid=(nb * spb,), in_specs=in_specs,
        out_specs=pl.BlockSpec((1, nq, KV_LORA_RANK), lambda t, pt: (t // spb, 0, 0)),
        scratch_shapes=[pltpu.VMEM((2, g * page, KV_LORA_RANK), F32), pltpu.VMEM((2, QK_ROPE_DIM, g * page), F32),
                        pltpu.SemaphoreType.DMA((2,)), pltpu.SemaphoreType.DMA((2,)),
                        pltpu.VMEM((nq, 1), F32), pltpu.VMEM((nq, 1), F32), pltpu.VMEM((nq, KV_LORA_RANK), F32)])
    return pl.pallas_call(
        functools.partial(_decode_kernel, layer=layer, n_pages_step=g, steps_per_batch=spb, n_split=n_split,
                          n_new=n_new),
        grid_spec=grid_spec, out_shape=jax.ShapeDtypeStruct((nb, nq, KV_LORA_RANK), F32),
        compiler_params=_params(1), name=name,
    )(page_table, qcat, c_new, kr_new, cache_c, cache_krt)


def _vup_kernel(o_ref, w_ref, a_ref):
    acc = _dot(o_ref[:, 0, :].astype(BF16), w_ref[0])
    for hh in range(1, N_HEADS):
        acc = acc + _dot(o_ref[:, hh, :].astype(BF16), w_ref[hh])
    a_ref[...] = acc.astype(BF16)


def _vup_call(o_lat, wuv_exp, *, name):
    rows = o_lat.shape[0]
    width = N_HEADS * V_HEAD_DIM
    return pl.pallas_call(
        _vup_kernel, grid=(1,),
        in_specs=[pl.BlockSpec((rows, N_HEADS, KV_LORA_RANK), lambda i: (0, 0, 0)),
                  pl.BlockSpec((N_HEADS, KV_LORA_RANK, width), lambda i: (0, 0, 0))],
        out_specs=pl.BlockSpec((rows, width), lambda i: (0, 0)),
        out_shape=jax.ShapeDtypeStruct((rows, width), BF16),
        compiler_params=_params(1), name=name,
    )(o_lat, wuv_exp)


def _merge_kernel(x_ref, a_ref, s_ref, m_ref, wg_ref, bg_ref, wbr_ref, wo_ref, g_ref, b_ref, o_ref, *, alpha):
    xf = x_ref[...]
    xb = xf.astype(BF16)
    merged = None
    for i, br in enumerate((a_ref, s_ref, m_ref)):
        gate = jax.nn.sigmoid(_dot(xb, wg_ref[:, i * D_MODEL:(i + 1) * D_MODEL]) + bg_ref[i:i + 1, :])
        term = gate * _dot(br[...], wbr_ref[i])
        merged = term if merged is None else merged + term
    y = alpha * xf + _dot(merged.astype(BF16), wo_ref[...])
    o_ref[...] = _layer_norm(y, g_ref[...], b_ref[...])


def _merge_call(x, a, s, m, lw, *, tm, alpha, name):
    rows = x.shape[0]

    def row_spec(n):
        return pl.BlockSpec((tm, n), lambda i: (i, 0))

    return pl.pallas_call(
        functools.partial(_merge_kernel, alpha=alpha), grid=(rows // tm,),
        in_specs=[row_spec(D_MODEL), row_spec(BRANCH_DIM), row_spec(BRANCH_DIM), row_spec(BRANCH_DIM),
                  _const_spec((D_MODEL, N_BRANCH * D_MODEL)), _const_spec((N_BRANCH, D_MODEL)),
                  _const_spec((N_BRANCH, BRANCH_DIM, D_MODEL)), _const_spec((D_MODEL, D_MODEL)),
                  _const_spec((1, D_MODEL)), _const_spec((1, D_MODEL))],
        out_specs=row_spec(D_MODEL),
        out_shape=jax.ShapeDtypeStruct((rows, D_MODEL), F32),
        compiler_params=_params(1), name=name,
    )(x, a, s, m, lw["wg"], lw["bg"], lw["wbr"], lw["wo"], lw["ln1_g"], lw["ln1_b"])


def _ffn_kernel(x_ref, w1_ref, w3_ref, w2_ref, g_ref, b_ref, o_ref, *, alpha, n_split):
    xf = x_ref[...]
    xb = xf.astype(BF16)
    d_ff = w1_ref.shape[1]
    step = d_ff // n_split
    ff = None
    for c in range(n_split):
        cs = slice(c * step, (c + 1) * step)
        act = (jax.nn.silu(_dot(xb, w1_ref[:, cs])) * _dot(xb, w3_ref[:, cs])).astype(BF16)
        part = _dot(act, w2_ref[cs, :])
        ff = part if ff is None else ff + part
    o_ref[...] = _layer_norm(alpha * xf + ff, g_ref[...], b_ref[...])


def _ffn_call(x, lw, *, tm, alpha, name):
    rows = x.shape[0]
    d_ff = lw["w_ff1"].shape[1]
    row_spec = pl.BlockSpec((tm, D_MODEL), lambda i: (i, 0))
    return pl.pallas_call(
        functools.partial(_ffn_kernel, alpha=alpha, n_split=2), grid=(rows // tm,),
        in_specs=[row_spec, _const_spec((D_MODEL, d_ff)), _const_spec((D_MODEL, d_ff)),
                  _const_spec((d_ff, D_MODEL)), _const_spec((1, D_MODEL)), _const_spec((1, D_MODEL))],
        out_specs=row_spec,
        out_shape=jax.ShapeDtypeStruct((rows, D_MODEL), F32),
        compiler_params=_params(1), name=name,
    )(x, lw["w_ff1"], lw["w_ff3"], lw["w_ff2"], lw["ln2_g"], lw["ln2_b"])


def _rot_cols(w):
    half = w.shape[-1] // 2
    return jnp.concatenate([-w[..., half:], w[..., :half]], axis=-1)


def _placement(rows, cols, pairs):
    e = np.zeros((rows, cols), np.float32)
    for r, c in pairs:
        e[r, c] = 1.0
    return jnp.asarray(e, dtype=BF16)


def _prep_layer(l, chunk_lens, w_in, b_gate, q_norm_g, kv_norm_g, w_uq, w_uk, w_uv, sgu_ln_g, sgu_ln_b, w_s, b_s,
                w_br, w_o, ln1_g, ln1_b, w_ff1, w_ff3, w_ff2, ln2_g, ln2_b):
    wi = w_in[l]
    wkr = wi[:, OFF_KR:OFF_U]
    zpad = jnp.zeros((D_MODEL, LANES - QK_ROPE_DIM), F32)
    w1 = jnp.concatenate([wi[:, OFF_CQ:OFF_KR], wkr, zpad, _rot_cols(wkr), zpad, wi[:, OFF_U:OFF_G]], axis=1)

    wuq = w_uq[l].reshape(Q_LORA_RANK, N_HEADS, QK_HEAD_DIM)
    wq_nope = jnp.pad(wuq[:, :, :QK_NOPE_DIM], ((0, 0), (0, 0), (0, HEAD_SLAB - QK_NOPE_DIM)))
    wq_rope = wuq[:, :, QK_NOPE_DIM:]
    wq = jnp.concatenate([wq_nope.reshape(Q_LORA_RANK, QK_SLAB), wq_rope.reshape(Q_LORA_RANK, ROPE_ALL),
                          _rot_cols(wq_rope).reshape(Q_LORA_RANK, ROPE_ALL)], axis=1)
    eq = _placement(ROPE_ALL, QK_SLAB, [(hh * QK_ROPE_DIM + d, hh * HEAD_SLAB + QK_NOPE_DIM + d)
                                        for hh in range(N_HEADS) for d in range(QK_ROPE_DIM)])
    ek = _placement(LANES, QK_SLAB, [(d, hh * HEAD_SLAB + QK_NOPE_DIM + d)
                                     for hh in range(N_HEADS) for d in range(QK_ROPE_DIM)])
    wuk = jnp.pad(w_uk[l], ((0, 0), (0, 0), (0, HEAD_SLAB - QK_NOPE_DIM))).reshape(KV_LORA_RANK, QK_SLAB)
    wuvt = jnp.pad(jnp.transpose(w_uv[l], (1, 2, 0)), ((0, 0), (0, VT_ROWS - V_HEAD_DIM), (0, 0)))
    wuvt = wuvt.reshape(VT_ALL, KV_LORA_RANK)
    vones = np.zeros((VT_ALL, 1), np.float32)
    vones[np.arange(N_HEADS) * VT_ROWS + V_HEAD_DIM] = 1.0

    wcat = jnp.zeros((N_HEADS, HEAD_SLAB, QCAT), F32)
    wcat = wcat.at[:, :QK_NOPE_DIM, :KV_LORA_RANK].set(jnp.transpose(w_uk[l], (1, 2, 0)))
    d = np.arange(QK_ROPE_DIM)
    wcat = wcat.at[:, QK_NOPE_DIM + d, KV_LORA_RANK + d].set(1.0)
    wuv_exp = jnp.zeros((N_HEADS, KV_LORA_RANK, N_HEADS * V_HEAD_DIM), F32)
    for hh in range(N_HEADS):
        wuv_exp = wuv_exp.at[hh, :, hh * V_HEAD_DIM:(hh + 1) * V_HEAD_DIM].set(w_uv[l][:, hh, :])

    lw = {
        "w1": w1.astype(BF16), "qg": q_norm_g[l][None, :], "kvg": kv_norm_g[l][None, :],
        "wq": wq.astype(BF16), "eq": eq, "ek": ek, "wuk": wuk.astype(BF16), "wuvt": wuvt.astype(BF16),
        "vones": jnp.asarray(vones),
        "sgu_g": sgu_ln_g[l][None, :], "sgu_b": sgu_ln_b[l][None, :],
        "wcat": wcat.astype(BF16), "wuv_exp": wuv_exp.astype(BF16),
        "wg": wi[:, OFF_G:].astype(BF16), "bg": b_gate[l], "wbr": w_br[l].astype(BF16), "wo": w_o[l].astype(BF16),
        "ln1_g": ln1_g[l][None, :], "ln1_b": ln1_b[l][None, :],
        "w_ff1": w_ff1[l].astype(BF16), "w_ff3": w_ff3[l].astype(BF16), "w_ff2": w_ff2[l].astype(BF16),
        "ln2_g": ln2_g[l][None, :], "ln2_b": ln2_b[l][None, :],
    }
    for key, cl in chunk_lens.items():
        reps = CHUNK // cl
        lw["wmix_" + key] = jnp.tile(w_s[l][:, :cl, :cl], (1, reps, reps)).astype(BF16)
        bias = jnp.tile(jnp.transpose(b_s[l][:, :cl]), (reps, 1))
        lw["mixbias_" + key] = jnp.repeat(bias, GROUP_DIM, axis=1)
    return lw


def _rope_tables(pos):
    half = QK_ROPE_DIM // 2
    inv_freq = jnp.power(ROPE_THETA, -jnp.arange(half, dtype=F32) / half)
    ang = pos.astype(F32)[:, None] * inv_freq[None, :]
    reps = LANES // half
    return jnp.tile(jnp.cos(ang), (1, reps)), jnp.tile(jnp.sin(ang), (1, reps))


def kernel(x_prompt, x_sample, mem_prompt, cache_kv_latent, cache_k_rope, cache_mem_k, cache_mem_v, page_table,
           w_in, b_gate, q_norm_g, kv_norm_g, w_uq, w_uk, w_uv, sgu_ln_g, sgu_ln_b, w_s, b_s, w_mk, w_mv,
           w_br, w_o, ln1_g, ln1_b, w_ff1, w_ff3, w_ff2, ln2_g, ln2_b):
    nbp, seq, _ = x_prompt.shape
    nbs, tdec, _ = x_sample.shape
    depth = w_in.shape[0]
    n_pages = page_table.shape[1]
    past = n_pages * cache_kv_latent.shape[2]
    alpha = (2 * depth) ** 0.25
    assert nbp == 1 and seq % CHUNK == 0 and tdec < CHUNK and CHUNK % tdec == 0

    cos_p, sin_p = _rope_tables(jnp.arange(seq, dtype=jnp.int32))
    cos_s, sin_s = _rope_tables(jnp.tile(past + jnp.arange(tdec, dtype=jnp.int32), nbs))
    chunk_lens = {"p": min(seq, CHUNK), "s": min(tdec, CHUNK)}

    mk_p_all, mv_p_all = _memkv_call(mem_prompt[0], w_mk.astype(BF16), w_mv.astype(BF16))
    mk_p4 = mk_p_all.reshape(depth, nbp, N_MEM * MEM_HEADS, MEM_HEAD_DIM)
    mv_p4 = mv_p_all.reshape(depth, nbp, N_MEM * MEM_HEADS, MEM_HEAD_DIM)
    cache_krt = jnp.swapaxes(cache_k_rope, 2, 3)
    cache_mk = cache_mem_k.reshape(depth, nbs, N_MEM * MEM_HEADS, MEM_HEAD_DIM)
    cache_mv = cache_mem_v.reshape(depth, nbs, N_MEM * MEM_HEADS, MEM_HEAD_DIM)

    xp = x_prompt.reshape(seq, D_MODEL)
    xs = x_sample.reshape(nbs * tdec, D_MODEL)
    ckv_p, kr_p, ckv_s, kr_s, v_s = [], [], [], [], []
    for l in range(depth):
        lw = _prep_layer(l, chunk_lens, w_in, b_gate, q_norm_g, kv_norm_g, w_uq, w_uk, w_uv, sgu_ln_g, sgu_ln_b,
                         w_s, b_s, w_br, w_o, ln1_g, ln1_b, w_ff1, w_ff3, w_ff2, ln2_g, ln2_b)

        lw_p = dict(lw, wmix=lw["wmix_p"], mixbias=lw["mixbias_p"])
        c_new, k_new, q, k, vt, s_out, xq = _proj_call(
            xp, cos_p, sin_p, lw_p, tm=512, chunk_len=chunk_lens["p"], with_kv=True, with_vln=False,
            name=f"proj_prompt_{l}")
        ckv_p.append(c_new)
        kr_p.append(k_new)
        a_out = _flash_call(q, k, vt, tq=512, tk=512, name=f"flash_prompt_{l}")
        m_out = _memattn_call(xq[None], mk_p4, mv_p4, layer=l, bt=1, ts=512, name=f"memattn_prompt_{l}")[0]
        x1 = _merge_call(xp, a_out, s_out, m_out, lw, tm=512, alpha=alpha, name=f"merge_prompt_{l}")
        xp = _ffn_call(x1, lw, tm=512, alpha=alpha, name=f"ffn_prompt_{l}")

        lw_s = dict(lw, wmix=lw["wmix_s"], mixbias=lw["mixbias_s"])
        c_new, k_new, q, vln, s_out, xq = _proj_call(
            xs, cos_s, sin_s, lw_s, tm=256, chunk_len=chunk_lens["s"], with_kv=False, with_vln=True,
            name=f"proj_sample_{l}")
        ckv_s.append(c_new)
        kr_s.append(k_new)
        v_s.append(vln)
        qcat = _qcat_call(q, lw["wcat"], name=f"qcat_sample_{l}")
        o_lat = _decode_call(page_table, qcat.reshape(nbs, tdec * N_HEADS, QCAT),
                             c_new.reshape(nbs, tdec, KV_LORA_RANK), k_new.reshape(nbs, tdec, QK_ROPE_DIM),
                             cache_kv_latent, cache_krt, layer=l, pages_per_step=64, n_split=4,
                             name=f"decode_sample_{l}")
        a_out = _vup_call(o_lat.reshape(nbs * tdec, N_HEADS, KV_LORA_RANK), lw["wuv_exp"], name=f"vup_sample_{l}")
        m_out = _memattn_call(xq.reshape(nbs, tdec, MEM_DIM), cache_mk, cache_mv, layer=l, bt=8, ts=tdec,
                              name=f"memattn_sample_{l}").reshape(nbs * tdec, MEM_DIM)
        x1 = _merge_call(xs, a_out, s_out, m_out, lw, tm=256, alpha=alpha, name=f"merge_sample_{l}")
        xs = _ffn_call(x1, lw, tm=256, alpha=alpha, name=f"ffn_sample_{l}")

    return (xp.reshape(nbp, seq, D_MODEL),
            xs.reshape(nbs, tdec, D_MODEL),
            jnp.stack(ckv_p).reshape(depth, nbp, seq, KV_LORA_RANK),
            jnp.stack(kr_p).reshape(depth, nbp, seq, QK_ROPE_DIM),
            mk_p_all.reshape(depth, nbp, N_MEM, MEM_HEADS, MEM_HEAD_DIM),
            mv_p_all.reshape(depth, nbp, N_MEM, MEM_HEADS, MEM_HEAD_DIM),
            jnp.stack(ckv_s).reshape(depth, nbs, tdec, KV_LORA_RANK),
            jnp.stack(kr_s).reshape(depth, nbs, tdec, QK_ROPE_DIM),
            jnp.stack(v_s).reshape(depth, nbs, tdec, SGU_DIM))
```

```python
import functools
import math

import numpy as np
import jax
import jax.numpy as jnp
from jax import lax
from jax.experimental import pallas as pl
from jax.experimental.pallas import tpu as pltpu

F32 = jnp.float32
BF16 = jnp.bfloat16

D_MODEL = 1024
N_HEADS = 8
QK_NOPE_DIM = 64
QK_ROPE_DIM = 32
QK_HEAD_DIM = QK_NOPE_DIM + QK_ROPE_DIM
V_HEAD_DIM = 64
Q_LORA_RANK = 384
KV_LORA_RANK = 256
ROPE_THETA = 10000.0
MLA_SCALE = QK_HEAD_DIM ** -0.5
CHUNK = 128
N_GROUPS = 4
GROUP_DIM = 128
SGU_DIM = N_GROUPS * GROUP_DIM
N_MEM = 256
MEM_HEADS = 4
MEM_HEAD_DIM = 128
MEM_DIM = MEM_HEADS * MEM_HEAD_DIM
MEM_SCALE = MEM_HEAD_DIM ** -0.5
N_BRANCH = 3
BRANCH_DIM = 512
LN_EPS = 1e-5
RMS_EPS = 1e-6
OFF_CQ = 0
OFF_CKV = OFF_CQ + Q_LORA_RANK
OFF_KR = OFF_CKV + KV_LORA_RANK
OFF_U = OFF_KR + QK_ROPE_DIM
OFF_V = OFF_U + SGU_DIM
OFF_XQ = OFF_V + SGU_DIM
OFF_G = OFF_XQ + MEM_DIM

LANES = 128
HEAD_SLAB = LANES
QK_SLAB = N_HEADS * HEAD_SLAB
ROPE_ALL = N_HEADS * QK_ROPE_DIM
P_CQ = 0
P_CKV = P_CQ + Q_LORA_RANK
P_KRA = P_CKV + KV_LORA_RANK
P_KRB = P_KRA + LANES
P_U = P_KRB + LANES
P_V = P_U + SGU_DIM
P_XQ = P_V + SGU_DIM
P_END = P_XQ + MEM_DIM
QCAT = 384
SUBLANES = 8
VT_ROWS = 80
VT_ALL = N_HEADS * VT_ROWS
EXP2_SCALE = MLA_SCALE * math.log2(math.e)
VMEM_LIMIT = 56 * 1024 * 1024


def _dot(a, b):
    return jnp.dot(a, b, preferred_element_type=F32)


def _dot_nt(a, b):
    return lax.dot_general(a, b, (((1,), (1,)), ((), ())), preferred_element_type=F32)


def _const_spec(shape):
    nd = len(shape)
    return pl.BlockSpec(shape, lambda *_: (0,) * nd, pipeline_mode=pl.Buffered(1))


def _params(n_axes):
    return pltpu.CompilerParams(dimension_semantics=("arbitrary",) * n_axes,
                                vmem_limit_bytes=VMEM_LIMIT)


def _layer_norm(x, g, b):
    mu = jnp.mean(x, axis=-1, keepdims=True)
    xc = x - mu
    var = jnp.mean(xc * xc, axis=-1, keepdims=True)
    return xc * lax.rsqrt(var + LN_EPS) * g + b


def _rms_norm(x, g):
    ms = jnp.mean(x * x, axis=-1, keepdims=True)
    return x * lax.rsqrt(ms + RMS_EPS) * g


def _proj_kernel(x_ref, cos_ref, sin_ref, w1_ref, qg_ref, kvg_ref, wq_ref, eq_ref, ek_ref, wuk_ref,
                 wuvt_ref, vones_ref, lng_ref, lnb_ref, wmix_ref, bias_ref, *out_refs, chunk_len, with_kv, with_vln):
    out_refs = list(out_refs)
    ckv_ref, kr_ref, q_ref = out_refs[:3]
    rest = out_refs[3:]
    if with_kv:
        k_ref, vt_ref = rest[:2]
        rest = rest[2:]
    if with_vln:
        vln_ref = rest[0]
        rest = rest[1:]
    s_ref, xq_ref = rest

    tm = x_ref.shape[0]
    xb = x_ref[...].astype(BF16)
    h = _dot(xb, w1_ref[...])
    cos = cos_ref[...]
    sin = sin_ref[...]

    cq = _rms_norm(h[:, P_CQ:P_CKV], qg_ref[...]).astype(BF16)
    qq = _dot(cq, wq_ref[...])
    q_full = qq[:, :QK_SLAB]
    for j in range(ROPE_ALL // LANES):
        qa = qq[:, QK_SLAB + j * LANES:QK_SLAB + (j + 1) * LANES]
        qb = qq[:, QK_SLAB + ROPE_ALL + j * LANES:QK_SLAB + ROPE_ALL + (j + 1) * LANES]
        qr = (qa * cos + qb * sin).astype(BF16)
        q_full = q_full + _dot(qr, eq_ref[j * LANES:(j + 1) * LANES, :])
    q_ref[...] = q_full.astype(BF16)

    ckv = _rms_norm(h[:, P_CKV:P_KRA], kvg_ref[...])
    ckv_ref[...] = ckv
    kr128 = h[:, P_KRA:P_KRB] * cos + h[:, P_KRB:P_U] * sin
    kr_ref[...] = kr128[:, :QK_ROPE_DIM]
    if with_kv:
        ckv_b = ckv.astype(BF16)
        k_full = _dot(ckv_b, wuk_ref[...]) + _dot(kr128.astype(BF16), ek_ref[...])
        k_ref[...] = k_full.astype(BF16)
        vt_ref[...] = (_dot_nt(wuvt_ref[...], ckv_b) + vones_ref[...]).astype(BF16)

    u = jax.nn.gelu(h[:, P_U:P_V])
    vln = _layer_norm(jax.nn.gelu(h[:, P_V:P_XQ]), lng_ref[...], lnb_ref[...])
    if with_vln:
        vln_ref[...] = vln
    vb = vln.astype(BF16)
    row = lax.broadcasted_iota(jnp.int32, (CHUNK, CHUNK), 0)
    col = lax.broadcasted_iota(jnp.int32, (CHUNK, CHUNK), 1)
    mask = col <= row
    if chunk_len < CHUNK:
        mask = mask & ((col // chunk_len) == (row // chunk_len))
    wm = [jnp.where(mask, wmix_ref[g], jnp.zeros((), BF16)) for g in range(N_GROUPS)]
    for c in range(tm // CHUNK):
        rs = slice(c * CHUNK, (c + 1) * CHUNK)
        for g in range(N_GROUPS):
            cs = slice(g * GROUP_DIM, (g + 1) * GROUP_DIM)
            mixed = _dot(wm[g], vb[rs, cs]) + bias_ref[:, cs]
            s_ref[rs, cs] = (u[rs, cs] * mixed).astype(BF16)

    xq_ref[...] = h[:, P_XQ:P_END].astype(BF16)


def _proj_call(x, cos, sin, lw, *, tm, chunk_len, with_kv, with_vln, name):
    rows = x.shape[0]
    grid = (rows // tm,)

    def row_spec(n):
        return pl.BlockSpec((tm, n), lambda i: (i, 0))

    in_specs = [row_spec(D_MODEL), row_spec(LANES), row_spec(LANES),
                _const_spec((D_MODEL, P_END)), _const_spec((1, Q_LORA_RANK)), _const_spec((1, KV_LORA_RANK)),
                _const_spec((Q_LORA_RANK, QK_SLAB + 2 * ROPE_ALL)), _const_spec((ROPE_ALL, QK_SLAB)),
                _const_spec((LANES, QK_SLAB)), _const_spec((KV_LORA_RANK, QK_SLAB)),
                _const_spec((VT_ALL, KV_LORA_RANK)), _const_spec((VT_ALL, 1)), _const_spec((1, SGU_DIM)),
                _const_spec((1, SGU_DIM)), _const_spec((N_GROUPS, CHUNK, CHUNK)), _const_spec((CHUNK, SGU_DIM))]
    out_shape = [jax.ShapeDtypeStruct((rows, KV_LORA_RANK), F32),
                 jax.ShapeDtypeStruct((rows, QK_ROPE_DIM), F32),
                 jax.ShapeDtypeStruct((rows, QK_SLAB), BF16)]
    out_specs = [row_spec(KV_LORA_RANK), row_spec(QK_ROPE_DIM), row_spec(QK_SLAB)]
    if with_kv:
        out_shape += [jax.ShapeDtypeStruct((rows, QK_SLAB), BF16),
                      jax.ShapeDtypeStruct((VT_ALL, rows), BF16)]
        out_specs += [row_spec(QK_SLAB), pl.BlockSpec((VT_ALL, tm), lambda i: (0, i))]
    if with_vln:
        out_shape += [jax.ShapeDtypeStruct((rows, SGU_DIM), F32)]
        out_specs += [row_spec(SGU_DIM)]
    out_shape += [jax.ShapeDtypeStruct((rows, SGU_DIM), BF16), jax.ShapeDtypeStruct((rows, MEM_DIM), BF16)]
    out_specs += [row_spec(SGU_DIM), row_spec(MEM_DIM)]

    body = functools.partial(_proj_kernel, chunk_len=chunk_len, with_kv=with_kv, with_vln=with_vln)
    return pl.pallas_call(
        body, grid=grid, in_specs=in_specs, out_specs=out_specs, out_shape=out_shape,
        compiler_params=_params(1), name=name,
    )(x, cos, sin, lw["w1"], lw["qg"], lw["kvg"], lw["wq"], lw["eq"], lw["ek"], lw["wuk"], lw["wuvt"],
      lw["vones"], lw["sgu_g"], lw["sgu_b"], lw["wmix"], lw["mixbias"])


def _memkv_kernel(mem_ref, wk_ref, wv_ref, mk_ref, mv_ref):
    mb = mem_ref[...].astype(BF16)
    mk_ref[...] = _dot(mb, wk_ref[...])
    mv_ref[...] = _dot(mb, wv_ref[...])


def _memkv_call(mem, wk, wv):
    depth = wk.shape[0]
    w_spec = pl.BlockSpec((None, D_MODEL, MEM_DIM), lambda l: (l, 0, 0))
    o_spec = pl.BlockSpec((None, N_MEM, MEM_DIM), lambda l: (l, 0, 0))
    return pl.pallas_call(
        _memkv_kernel, grid=(depth,),
        in_specs=[pl.BlockSpec((N_MEM, D_MODEL), lambda l: (0, 0)), w_spec, w_spec],
        out_specs=[o_spec, o_spec],
        out_shape=[jax.ShapeDtypeStruct((depth, N_MEM, MEM_DIM), F32)] * 2,
        compiler_params=_params(1), name="mem_kv",
    )(mem, wk, wv)


def _memattn_kernel(xq_ref, mk_ref, mv_ref, o_ref, *, bt):
    pairs = [(b, hh) for b in range(bt) for hh in range(MEM_HEADS)]

    def head_rows(ref, b, hh):
        return ref[b, pl.ds(hh, N_MEM, stride=MEM_HEADS), :].astype(BF16)

    def cols(hh):
        return slice(hh * MEM_HEAD_DIM, (hh + 1) * MEM_HEAD_DIM)

    scores = [_dot_nt(xq_ref[b, :, cols(hh)], head_rows(mk_ref, b, hh)) * MEM_SCALE for b, hh in pairs]
    probs = []
    for s in scores:
        e = jnp.exp(s - jnp.max(s, axis=-1, keepdims=True))
        probs.append((e / jnp.sum(e, axis=-1, keepdims=True)).astype(BF16))
    for (b, hh), p in zip(pairs, probs):
        o_ref[b, :, cols(hh)] = _dot(p, head_rows(mv_ref, b, hh)).astype(BF16)


def _memattn_call(xq, mk, mv, *, layer, bt, ts, name):
    nb, s, _ = xq.shape
    kv_spec = pl.BlockSpec((None, bt, N_MEM * MEM_HEADS, MEM_HEAD_DIM), lambda b, i: (layer, b, 0, 0))
    x_spec = pl.BlockSpec((bt, ts, MEM_DIM), lambda b, i: (b, i, 0))
    return pl.pallas_call(
        functools.partial(_memattn_kernel, bt=bt), grid=(nb // bt, s // ts),
        in_specs=[x_spec, kv_spec, kv_spec], out_specs=x_spec,
        out_shape=jax.ShapeDtypeStruct((nb, s, MEM_DIM), BF16),
        compiler_params=_params(2), name=name,
    )(xq, mk, mv)


def _flash_kernel(qtab_ref, ktab_ref, q_ref, k_ref, vt_ref, o_ref, m_ref, acc_ref, *, tq, nsub):
    tk = tq
    t = pl.program_id(0)
    qi = qtab_ref[t]
    ki = ktab_ref[t]

    @pl.when(ki == 0)
    def _():
        m_ref[...] = jnp.full(m_ref.shape, -jnp.inf, F32)
        acc_ref[...] = jnp.zeros(acc_ref.shape, F32)

    def sub_block(sub, masked):
        k0 = pl.multiple_of(sub * tk, tk)

        def scores(hh):
            cs = slice(hh * HEAD_SLAB, (hh + 1) * HEAD_SLAB)
            return _dot_nt(k_ref[pl.ds(k0, tk), cs], q_ref[:, cs])

        keep = None
        if masked:
            key = lax.broadcasted_iota(jnp.int32, (tk, tq), 0)
            qry = lax.broadcasted_iota(jnp.int32, (tk, tq), 1)
            keep = key <= qry

        def update(hh, st):
            if keep is not None:
                st = jnp.where(keep, st, -jnp.inf)
            s3 = st.reshape(tk // SUBLANES, SUBLANES, tq)
            m_prev = m_ref[hh]
            m_cur = jnp.max(jnp.max(s3, axis=0), axis=0, keepdims=True)
            m_new = jnp.maximum(m_prev, m_cur)
            alpha = jnp.exp2((m_prev - m_new) * EXP2_SCALE)
            p = jnp.exp2((s3 - m_new[None]) * EXP2_SCALE).reshape(tk, tq).astype(BF16)
            rs = slice(hh * VT_ROWS, (hh + 1) * VT_ROWS)
            acc_ref[rs, :] = alpha[0:1] * acc_ref[rs, :] + _dot(vt_ref[rs, pl.ds(k0, tk)], p)
            m_ref[hh] = m_new

        pending = {0: scores(0), 1: scores(1)}
        for hh in range(N_HEADS):
            if hh + 2 < N_HEADS:
                pending[hh + 2] = scores(hh + 2)
            update(hh, pending.pop(hh))

    n_full = jnp.clip(qi - ki * nsub, 0, nsub)

    def full_body(sub, carry):
        sub_block(sub, False)
        return carry

    lax.fori_loop(0, n_full, full_body, 0)

    @pl.when(n_full < nsub)
    def _():
        sub_block(n_full, True)
        outs = []
        for hh in range(N_HEADS):
            base = hh * VT_ROWS
            inv = 1.0 / acc_ref[base + V_HEAD_DIM:base + V_HEAD_DIM + 1, :]
            outs.append(acc_ref[base:base + V_HEAD_DIM, :] * inv)
        o_ref[...] = jnp.concatenate(outs, axis=0).T.astype(BF16)


def _flash_call(q, k, vt, *, tq, nsub, name):
    s = q.shape[0]
    tkb = tq * nsub
    assert s % tkb == 0
    steps =[(qi, ki) for qi in range(s // tq) for ki in range(qi // nsub + 1)]
    qtab = jnp.asarray(np.array([p[0] for p in steps], np.int32))
    ktab = jnp.asarray(np.array([p[1] for p in steps], np.int32))
    width = N_HEADS * V_HEAD_DIM
    grid_spec = pltpu.PrefetchScalarGridSpec(
        num_scalar_prefetch=2, grid=(len(steps),),
        in_specs=[pl.BlockSpec((tq, QK_SLAB), lambda t, qt, kt: (qt[t], 0)),
                  pl.BlockSpec((tkb, QK_SLAB), lambda t, qt, kt: (kt[t], 0)),
                  pl.BlockSpec((VT_ALL, tkb), lambda t, qt, kt: (0, kt[t]))],
        out_specs=pl.BlockSpec((tq, width), lambda t, qt, kt: (qt[t], 0)),
        scratch_shapes=[pltpu.VMEM((N_HEADS, SUBLANES, tq), F32), pltpu.VMEM((VT_ALL, tq), F32)])
    return pl.pallas_call(
        functools.partial(_flash_kernel, tq=tq, nsub=nsub), grid_spec=grid_spec,
        out_shape=jax.ShapeDtypeStruct((s, width), BF16),
        compiler_params=_params(1), name=name,
    )(qtab, ktab, q, k, vt)


def _qcat_kernel(q_ref, w_ref, o_ref):
    for hh in range(N_HEADS):
        o_ref[:, hh, :] = _dot(q_ref[:, hh * HEAD_SLAB:(hh + 1) * HEAD_SLAB], w_ref[hh])


def _qcat_call(q, wcat, *, name):
    rows = q.shape[0]
    return pl.pallas_call(
        _qcat_kernel, grid=(1,),
        in_specs=[pl.BlockSpec((rows, QK_SLAB), lambda i: (0, 0)),
                  pl.BlockSpec((N_HEADS, HEAD_SLAB, QCAT), lambda i: (0, 0, 0))],
        out_specs=pl.BlockSpec((rows, N_HEADS, QCAT), lambda i: (0, 0, 0)),
        out_shape=jax.ShapeDtypeStruct((rows, N_HEADS, QCAT), F32),
        compiler_params=_params(1), name=name,
    )(q, wcat)


def _decode_kernel(pt_ref, q_ref, cnew_ref, krnew_ref, cache_c_ref, cache_krt_ref, o_ref,
                   cbuf, krbuf, csem, krsem, m_ref, l_ref, acc_ref, *, layer, n_pages_step, steps_per_batch, n_split, n_ahead, n_new):
    g = n_pages_step
    page = cbuf.shape[1] // g
    t = pl.program_id(0)
    j = t % steps_per_batch
    slot = t % 2
    nq = q_ref.shape[1]

    def page_copies(step, buf):
        b = step // steps_per_batch
        first = (step % steps_per_batch) * g
        copies = []
        for i in range(g):
            pg = pt_ref[b, first + i]
            copies.append(pltpu.make_async_copy(
                cache_c_ref.at[layer, pg], cbuf.at[buf, pl.ds(i * page, page), :], csem.at[buf]))
            copies.append(pltpu.make_async_copy(
                cache_krt_ref.at[layer, pg], krbuf.at[buf, :, pl.ds(i * page, page)], krsem.at[buf]))
        return copies

    @pl.when(t == 0)
    def _():
        for cp in page_copies(t, slot):
            cp.start()

    @pl.when(t + 1 < pl.num_programs(0))
    def _():
        for cp in page_copies(t + 1, 1 - slot):
            cp.start()

    @pl.when(j == 0)
    def _():
        m_ref[...] = jnp.full(m_ref.shape, -jnp.inf, F32)
        l_ref[...] = jnp.zeros(l_ref.shape, F32)
        acc_ref[...] = jnp.zeros(acc_ref.shape, F32)

    q = q_ref[0]
    ql = q[:, :KV_LORA_RANK].astype(BF16)
    qr = q[:, KV_LORA_RANK:KV_LORA_RANK + QK_ROPE_DIM].astype(BF16)

    for cp in page_copies(t, slot):
        cp.wait()

    keys = (g * page) // n_split

    def chunk_scores(c):
        kb = cbuf[slot, pl.ds(c * keys, keys), :].astype(BF16)
        krt = krbuf[slot, :, pl.ds(c * keys, keys)].astype(BF16)
        return kb, (_dot_nt(ql, kb) + _dot(qr, krt)) * MLA_SCALE

    m_run = m_ref[...]
    l_run = l_ref[...]
    acc = acc_ref[...]
    pending = {c: chunk_scores(c) for c in range(min(n_ahead, n_split))}
    for c in range(n_split):
        if c + n_ahead < n_split:
            pending[c + n_ahead] = chunk_scores(c + n_ahead)
        kb, s = pending.pop(c)
        m_new = jnp.maximum(m_run, jnp.max(s, axis=-1, keepdims=True))
        alpha = jnp.exp(m_run - m_new)
        p = jnp.exp(s - m_new)
        l_run = alpha * l_run + jnp.sum(p, axis=-1, keepdims=True)
        acc = alpha * acc + _dot(p.astype(BF16), kb)
        m_run = m_new
    m_ref[...] = m_run
    l_ref[...] = l_run
    acc_ref[...] = acc

    @pl.when(j == steps_per_batch - 1)
    def _():
        qlf = ql.astype(F32)
        qrf = qr.astype(F32)
        tok = lax.broadcasted_iota(jnp.int32, (nq, 1), 0) // N_HEADS
        s_new = []
        cn = []
        for t in range(n_new):
            c_t = cnew_ref[0, t:t + 1, :].astype(BF16).astype(F32)
            kr_t = krnew_ref[0, t:t + 1, :].astype(BF16).astype(F32)
            s_t = (jnp.sum(qlf * c_t, axis=-1, keepdims=True)
                   + jnp.sum(qrf * kr_t, axis=-1, keepdims=True)) * MLA_SCALE
            s_new.append(jnp.where(tok >= t, s_t, -jnp.inf))
            cn.append(c_t)
        m_prev = m_ref[...]
        m_fin = m_prev
        for s_t in s_new:
            m_fin = jnp.maximum(m_fin, s_t)
        alpha = jnp.exp(m_prev - m_fin)
        l_fin = alpha * l_ref[...]
        acc = alpha * acc_ref[...]
        for t in range(n_new):
            p_t = jnp.exp(s_new[t] - m_fin)
            l_fin = l_fin + p_t
            acc = acc + p_t.astype(BF16).astype(F32) * cn[t]
        o_ref[0] = acc * (1.0 / l_fin)


def _decode_call(page_table, qcat, c_new, kr_new, cache_c, cache_krt, *, layer, pages_per_step, n_split, n_ahead,
                 name):
    nb, nq, _ = qcat.shape
    n_pages = page_table.shape[1]
    page = cache_c.shape[2]
    g = pages_per_step
    spb = n_pages // g
    n_new = c_new.shape[1]
    in_specs = [pl.BlockSpec((1, nq, QCAT), lambda t, pt: (t // spb, 0, 0)),
                pl.BlockSpec((1, n_new, KV_LORA_RANK), lambda t, pt: (t // spb, 0, 0)),
                pl.BlockSpec((1, n_new, QK_ROPE_DIM), lambda t, pt: (t // spb, 0, 0)),
                pl.BlockSpec(memory_space=pl.ANY), pl.BlockSpec(memory_space=pl.ANY)]
    grid_spec = pltpu.PrefetchScalarGridSpec(
        num_scalar_prefetch=1, grid=(nb * spb,), in_specs=in_specs,
        out_specs=pl.BlockSpec((1, nq, KV_LORA_RANK), lambda t, pt: (t // spb, 0, 0)),
        scratch_shapes=[pltpu.VMEM((2, g * page, KV_LORA_RANK), F32), pltpu.VMEM((2, QK_ROPE_DIM, g * page), F32),
                        pltpu.SemaphoreType.DMA((2,)), pltpu.SemaphoreType.DMA((2,)),
                        pltpu.VMEM((nq, 1), F32), pltpu.VMEM((nq, 1), F32), pltpu.VMEM((nq, KV_LORA_RANK), F32)])
    return pl.pallas_call(
        functools.partial(_decode_kernel, layer=layer, n_pages_step=g, steps_per_batch=spb, n_split=n_split, n_ahead=n_ahead,
                          n_new=n_new),
        grid_spec=grid_spec, out_shape=jax.ShapeDtypeStruct((nb, nq, KV_LORA_RANK), F32),
        compiler_params=_params(1), name=name,
    )(page_table, qcat, c_new, kr_new, cache_c, cache_krt)


def _vup_kernel(o_ref, w_ref, a_ref):
    acc = _dot(o_ref[:, 0, :].astype(BF16), w_ref[0])
    for hh in range(1, N_HEADS):
        acc = acc + _dot(o_ref[:, hh, :].astype(BF16), w_ref[hh])
    a_ref[...] = acc.astype(BF16)


def _vup_call(o_lat, wuv_exp, *, name):
    rows = o_lat.shape[0]
    width = N_HEADS * V_HEAD_DIM
    return pl.pallas_call(
        _vup_kernel, grid=(1,),
        in_specs=[pl.BlockSpec((rows, N_HEADS, KV_LORA_RANK), lambda i: (0, 0, 0)),
                  pl.BlockSpec((N_HEADS, KV_LORA_RANK, width), lambda i: (0, 0, 0))],
        out_specs=pl.BlockSpec((rows, width), lambda i: (0, 0)),
        out_shape=jax.ShapeDtypeStruct((rows, width), BF16),
        compiler_params=_params(1), name=name,
    )(o_lat, wuv_exp)


def _merge_kernel(x_ref, a_ref, s_ref, m_ref, wg_ref, bg_ref, wbr_ref, wo_ref, g_ref, b_ref, o_ref, *, alpha):
    xf = x_ref[...]
    xb = xf.astype(BF16)
    merged = None
    for i, br in enumerate((a_ref, s_ref, m_ref)):
        gate = jax.nn.sigmoid(_dot(xb, wg_ref[:, i * D_MODEL:(i + 1) * D_MODEL]) + bg_ref[i:i + 1, :])
        term = gate * _dot(br[...], wbr_ref[i])
        merged = term if merged is None else merged + term
    y = alpha * xf + _dot(merged.astype(BF16), wo_ref[...])
    o_ref[...] = _layer_norm(y, g_ref[...], b_ref[...])


def _merge_call(x, a, s, m, lw, *, tm, alpha, name):
    rows = x.shape[0]

    def row_spec(n):
        return pl.BlockSpec((tm, n), lambda i: (i, 0))

    return pl.pallas_call(
        functools.partial(_merge_kernel, alpha=alpha), grid=(rows // tm,),
        in_specs=[row_spec(D_MODEL), row_spec(BRANCH_DIM), row_spec(BRANCH_DIM), row_spec(BRANCH_DIM),
                  _const_spec((D_MODEL, N_BRANCH * D_MODEL)), _const_spec((N_BRANCH, D_MODEL)),
                  _const_spec((N_BRANCH, BRANCH_DIM, D_MODEL)), _const_spec((D_MODEL, D_MODEL)),
                  _const_spec((1, D_MODEL)), _const_spec((1, D_MODEL))],
        out_specs=row_spec(D_MODEL),
        out_shape=jax.ShapeDtypeStruct((rows, D_MODEL), F32),
        compiler_params=_params(1), name=name,
    )(x, a, s, m, lw["wg"], lw["bg"], lw["wbr"], lw["wo"], lw["ln1_g"], lw["ln1_b"])


def _ffn_kernel(x_ref, w1_ref, w3_ref, w2_ref, g_ref, b_ref, o_ref, *, alpha, n_split):
    xf = x_ref[...]
    xb = xf.astype(BF16)
    d_ff = w1_ref.shape[1]
    step = d_ff // n_split
    ff = None
    for c in range(n_split):
        cs = slice(c * step, (c + 1) * step)
        act = (jax.nn.silu(_dot(xb, w1_ref[:, cs])) * _dot(xb, w3_ref[:, cs])).astype(BF16)
        part = _dot(act, w2_ref[cs, :])
        ff = part if ff is None else ff + part
    o_ref[...] = _layer_norm(alpha * xf + ff, g_ref[...], b_ref[...])


def _ffn_call(x, lw, *, tm, alpha, name):
    rows = x.shape[0]
    d_ff = lw["w_ff1"].shape[1]
    row_spec = pl.BlockSpec((tm, D_MODEL), lambda i: (i, 0))
    return pl.pallas_call(
        functools.partial(_ffn_kernel, alpha=alpha, n_split=2), grid=(rows // tm,),
        in_specs=[row_spec, _const_spec((D_MODEL, d_ff)), _const_spec((D_MODEL, d_ff)),
                  _const_spec((d_ff, D_MODEL)), _const_spec((1, D_MODEL)), _const_spec((1, D_MODEL))],
        out_specs=row_spec,
        out_shape=jax.ShapeDtypeStruct((rows, D_MODEL), F32),
        compiler_params=_params(1), name=name,
    )(x, lw["w_ff1"], lw["w_ff3"], lw["w_ff2"], lw["ln2_g"], lw["ln2_b"])


def _rot_cols(w):
    half = w.shape[-1] // 2
    return jnp.concatenate([-w[..., half:], w[..., :half]], axis=-1)


def _placement(rows, cols, pairs):
    e = np.zeros((rows, cols), np.float32)
    for r, c in pairs:
        e[r, c] = 1.0
    return jnp.asarray(e, dtype=BF16)


def _prep_layer(l, chunk_lens, w_in, b_gate, q_norm_g, kv_norm_g, w_uq, w_uk, w_uv, sgu_ln_g, sgu_ln_b, w_s, b_s,
                w_br, w_o, ln1_g, ln1_b, w_ff1, w_ff3, w_ff2, ln2_g, ln2_b):
    wi = w_in[l]
    wkr = wi[:, OFF_KR:OFF_U]
    zpad = jnp.zeros((D_MODEL, LANES - QK_ROPE_DIM), F32)
    w1 = jnp.concatenate([wi[:, OFF_CQ:OFF_KR], wkr, zpad, _rot_cols(wkr), zpad, wi[:, OFF_U:OFF_G]], axis=1)

    wuq = w_uq[l].reshape(Q_LORA_RANK, N_HEADS, QK_HEAD_DIM)
    wq_nope = jnp.pad(wuq[:, :, :QK_NOPE_DIM], ((0, 0), (0, 0), (0, HEAD_SLAB - QK_NOPE_DIM)))
    wq_rope = wuq[:, :, QK_NOPE_DIM:]
    wq = jnp.concatenate([wq_nope.reshape(Q_LORA_RANK, QK_SLAB), wq_rope.reshape(Q_LORA_RANK, ROPE_ALL),
                          _rot_cols(wq_rope).reshape(Q_LORA_RANK, ROPE_ALL)], axis=1)
    eq = _placement(ROPE_ALL, QK_SLAB, [(hh * QK_ROPE_DIM + d, hh * HEAD_SLAB + QK_NOPE_DIM + d)
                                        for hh in range(N_HEADS) for d in range(QK_ROPE_DIM)])
    ek = _placement(LANES, QK_SLAB, [(d, hh * HEAD_SLAB + QK_NOPE_DIM + d)
                                     for hh in range(N_HEADS) for d in range(QK_ROPE_DIM)])
    wuk = jnp.pad(w_uk[l], ((0, 0), (0, 0), (0, HEAD_SLAB - QK_NOPE_DIM))).reshape(KV_LORA_RANK, QK_SLAB)
    wuvt = jnp.pad(jnp.transpose(w_uv[l], (1, 2, 0)), ((0, 0), (0, VT_ROWS - V_HEAD_DIM), (0, 0)))
    wuvt = wuvt.reshape(VT_ALL, KV_LORA_RANK)
    vones = np.zeros((VT_ALL, 1), np.float32)
    vones[np.arange(N_HEADS) * VT_ROWS + V_HEAD_DIM] = 1.0

    wcat = jnp.zeros((N_HEADS, HEAD_SLAB, QCAT), F32)
    wcat = wcat.at[:, :QK_NOPE_DIM, :KV_LORA_RANK].set(jnp.transpose(w_uk[l], (1, 2, 0)))
    d = np.arange(QK_ROPE_DIM)
    wcat = wcat.at[:, QK_NOPE_DIM + d, KV_LORA_RANK + d].set(1.0)
    wuv_exp = jnp.zeros((N_HEADS, KV_LORA_RANK, N_HEADS * V_HEAD_DIM), F32)
    for hh in range(N_HEADS):
        wuv_exp = wuv_exp.at[hh, :, hh * V_HEAD_DIM:(hh + 1) * V_HEAD_DIM].set(w_uv[l][:, hh, :])

    lw = {
        "w1": w1.astype(BF16), "qg": q_norm_g[l][None, :], "kvg": kv_norm_g[l][None, :],
        "wq": wq.astype(BF16), "eq": eq, "ek": ek, "wuk": wuk.astype(BF16), "wuvt": wuvt.astype(BF16),
        "vones": jnp.asarray(vones),
        "sgu_g": sgu_ln_g[l][None, :], "sgu_b": sgu_ln_b[l][None, :],
        "wcat": wcat.astype(BF16), "wuv_exp": wuv_exp.astype(BF16),
        "wg": wi[:, OFF_G:].astype(BF16), "bg": b_gate[l], "wbr": w_br[l].astype(BF16), "wo": w_o[l].astype(BF16),
        "ln1_g": ln1_g[l][None, :], "ln1_b": ln1_b[l][None, :],
        "w_ff1": w_ff1[l].astype(BF16), "w_ff3": w_ff3[l].astype(BF16), "w_ff2": w_ff2[l].astype(BF16),
        "ln2_g": ln2_g[l][None, :], "ln2_b": ln2_b[l][None, :],
    }
    for key, cl in chunk_lens.items():
        reps = CHUNK // cl
        lw["wmix_" + key] = jnp.tile(w_s[l][:, :cl, :cl], (1, reps, reps)).astype(BF16)
        bias = jnp.tile(jnp.transpose(b_s[l][:, :cl]), (reps, 1))
        lw["mixbias_" + key] = jnp.repeat(bias, GROUP_DIM, axis=1)
    return lw


def _rope_tables(pos):
    half = QK_ROPE_DIM // 2
    inv_freq = jnp.power(ROPE_THETA, -jnp.arange(half, dtype=F32) / half)
    ang = pos.astype(F32)[:, None] * inv_freq[None, :]
    reps = LANES // half
    return jnp.tile(jnp.cos(ang), (1, reps)), jnp.tile(jnp.sin(ang), (1, reps))


def kernel(x_prompt, x_sample, mem_prompt, cache_kv_latent, cache_k_rope, cache_mem_k, cache_mem_v, page_table,
           w_in, b_gate, q_norm_g, kv_norm_g, w_uq, w_uk, w_uv, sgu_ln_g, sgu_ln_b, w_s, b_s, w_mk, w_mv,
           w_br, w_o, ln1_g, ln1_b, w_ff1, w_ff3, w_ff2, ln2_g, ln2_b):
    nbp, seq, _ = x_prompt.shape
    nbs, tdec, _ = x_sample.shape
    depth = w_in.shape[0]
    n_pages = page_table.shape[1]
    past = n_pages * cache_kv_latent.shape[2]
    alpha = (2 * depth) ** 0.25
    assert nbp == 1 and seq % CHUNK == 0 and tdec < CHUNK and CHUNK % tdec == 0

    cos_p, sin_p = _rope_tables(jnp.arange(seq, dtype=jnp.int32))
    cos_s, sin_s = _rope_tables(jnp.tile(past + jnp.arange(tdec, dtype=jnp.int32), nbs))
    chunk_lens = {"p": min(seq, CHUNK), "s": min(tdec, CHUNK)}

    mk_p_all, mv_p_all = _memkv_call(mem_prompt[0], w_mk.astype(BF16), w_mv.astype(BF16))
    mk_p4 = mk_p_all.reshape(depth, nbp, N_MEM * MEM_HEADS, MEM_HEAD_DIM)
    mv_p4 = mv_p_all.reshape(depth, nbp, N_MEM * MEM_HEADS, MEM_HEAD_DIM)
    cache_krt = jnp.swapaxes(cache_k_rope, 2, 3)
    cache_mk = cache_mem_k.reshape(depth, nbs, N_MEM * MEM_HEADS, MEM_HEAD_DIM)
    cache_mv = cache_mem_v.reshape(depth, nbs, N_MEM * MEM_HEADS, MEM_HEAD_DIM)

    xp = x_prompt.reshape(seq, D_MODEL)
    xs = x_sample.reshape(nbs * tdec, D_MODEL)
    ckv_p, kr_p, ckv_s, kr_s, v_s = [], [], [], [], []
    for l in range(depth):
        lw = _prep_layer(l, chunk_lens, w_in, b_gate, q_norm_g, kv_norm_g, w_uq, w_uk, w_uv, sgu_ln_g, sgu_ln_b,
                         w_s, b_s, w_br, w_o, ln1_g, ln1_b, w_ff1, w_ff3, w_ff2, ln2_g, ln2_b)

        lw_p = dict(lw, wmix=lw["wmix_p"], mixbias=lw["mixbias_p"])
        c_new, k_new, q, k, vt, s_out, xq = _proj_call(
            xp, cos_p, sin_p, lw_p, tm=512, chunk_len=chunk_lens["p"], with_kv=True, with_vln=False,
            name=f"proj_prompt_{l}")
        ckv_p.append(c_new)
        kr_p.append(k_new)
        a_out = _flash_call(q, k, vt, tq=512, nsub=4, name=f"flash_prompt_{l}")
        m_out = _memattn_call(xq[None], mk_p4, mv_p4, layer=l, bt=1, ts=512, name=f"memattn_prompt_{l}")[0]
        x1 = _merge_call(xp, a_out, s_out, m_out, lw, tm=512, alpha=alpha, name=f"merge_prompt_{l}")
        xp = _ffn_call(x1, lw, tm=512, alpha=alpha, name=f"ffn_prompt_{l}")

        lw_s = dict(lw, wmix=lw["wmix_s"], mixbias=lw["mixbias_s"])
        c_new, k_new, q, vln, s_out, xq = _proj_call(
            xs, cos_s, sin_s, lw_s, tm=256, chunk_len=chunk_lens["s"], with_kv=False, with_vln=True,
            name=f"proj_sample_{l}")
        ckv_s.append(c_new)
        kr_s.append(k_new)
        v_s.append(vln)
        qcat = _qcat_call(q, lw["wcat"], name=f"qcat_sample_{l}")
        o_lat = _decode_call(page_table, qcat.reshape(nbs, tdec * N_HEADS, QCAT),
                             c_new.reshape(nbs, tdec, KV_LORA_RANK), k_new.reshape(nbs, tdec, QK_ROPE_DIM),
                             cache_kv_latent, cache_krt, layer=l, pages_per_step=128, n_split=8, n_ahead=4,
                             name=f"decode_sample_{l}")
        a_out = _vup_call(o_lat.reshape(nbs * tdec, N_HEADS, KV_LORA_RANK), lw["wuv_exp"], name=f"vup_sample_{l}")
        m_out = _memattn_call(xq.reshape(nbs, tdec, MEM_DIM), cache_mk, cache_mv, layer=l, bt=8, ts=tdec,
                              name=f"memattn_sample_{l}").reshape(nbs * tdec, MEM_DIM)
        x1 = _merge_call(xs, a_out, s_out, m_out, lw, tm=256, alpha=alpha, name=f"merge_sample_{l}")
        xs = _ffn_call(x1, lw, tm=256, alpha=alpha, name=f"ffn_sample_{l}")

    return (xp.reshape(nbp, seq, D_MODEL),
            xs.reshape(nbs, tdec, D_MODEL),
            jnp.stack(ckv_p).reshape(depth, nbp, seq, KV_LORA_RANK),
            jnp.stack(kr_p).reshape(depth, nbp, seq, QK_ROPE_DIM),
            mk_p_all.reshape(depth, nbp, N_MEM, MEM_HEADS, MEM_HEAD_DIM),
            mv_p_all.reshape(depth, nbp, N_MEM, MEM_HEADS, MEM_HEAD_DIM),
            jnp.stack(ckv_s).reshape(depth, nbs, tdec, KV_LORA_RANK),
            jnp.stack(kr_s).reshape(depth, nbs, tdec, QK_ROPE_DIM),
            jnp.stack(v_s).reshape(depth, nbs, tdec, SGU_DIM))
```

```python
import functools
import math

import numpy as np
import jax
import jax.numpy as jnp
from jax import lax
from jax.experimental import pallas as pl
from jax.experimental.pallas import tpu as pltpu

F32 = jnp.float32
BF16 = jnp.bfloat16

D_MODEL = 1024
N_HEADS = 8
QK_NOPE_DIM = 64
QK_ROPE_DIM = 32
QK_HEAD_DIM = QK_NOPE_DIM + QK_ROPE_DIM
V_HEAD_DIM = 64
Q_LORA_RANK = 384
KV_LORA_RANK = 256
ROPE_THETA = 10000.0
MLA_SCALE = QK_HEAD_DIM ** -0.5
CHUNK = 128
N_GROUPS = 4
GROUP_DIM = 128
SGU_DIM = N_GROUPS * GROUP_DIM
N_MEM = 256
MEM_HEADS = 4
MEM_HEAD_DIM = 128
MEM_DIM = MEM_HEADS * MEM_HEAD_DIM
MEM_SCALE = MEM_HEAD_DIM ** -0.5
N_BRANCH = 3
BRANCH_DIM = 512
LN_EPS = 1e-5
RMS_EPS = 1e-6
OFF_CQ = 0
OFF_CKV = OFF_CQ + Q_LORA_RANK
OFF_KR = OFF_CKV + KV_LORA_RANK
OFF_U = OFF_KR + QK_ROPE_DIM
OFF_V = OFF_U + SGU_DIM
OFF_XQ = OFF_V + SGU_DIM
OFF_G = OFF_XQ + MEM_DIM

LANES = 128
HEAD_SLAB = LANES
QK_SLAB = N_HEADS * HEAD_SLAB
ROPE_ALL = N_HEADS * QK_ROPE_DIM
P_CQ = 0
P_CKV = P_CQ + Q_LORA_RANK
P_KRA = P_CKV + KV_LORA_RANK
P_KRB = P_KRA + LANES
P_U = P_KRB + LANES
P_V = P_U + SGU_DIM
P_XQ = P_V + SGU_DIM
P_END = P_XQ + MEM_DIM
QCAT = 384
SUBLANES = 8
VT_ROWS = 80
VT_ALL = N_HEADS * VT_ROWS
EXP2_SCALE = MLA_SCALE * math.log2(math.e)
VMEM_LIMIT = 56 * 1024 * 1024


def _dot(a, b):
    return jnp.dot(a, b, preferred_element_type=F32)


def _dot_nt(a, b):
    return lax.dot_general(a, b, (((1,), (1,)), ((), ())), preferred_element_type=F32)


def _const_spec(shape):
    nd = len(shape)
    return pl.BlockSpec(shape, lambda *_: (0,) * nd, pipeline_mode=pl.Buffered(1))


def _layer_spec(shape, layer):
    nd = len(shape)
    return pl.BlockSpec((None,) + tuple(shape), lambda *_: (layer,) + (0,) * nd, pipeline_mode=pl.Buffered(1))


def _params(n_axes):
    return pltpu.CompilerParams(dimension_semantics=("arbitrary",) * n_axes,
                                vmem_limit_bytes=VMEM_LIMIT)


def _layer_norm(x, g, b):
    mu = jnp.mean(x, axis=-1, keepdims=True)
    xc = x - mu
    var = jnp.mean(xc * xc, axis=-1, keepdims=True)
    return xc * lax.rsqrt(var + LN_EPS) * g + b


def _rms_norm(x, g):
    ms = jnp.mean(x * x, axis=-1, keepdims=True)
    return x * lax.rsqrt(ms + RMS_EPS) * g


def _proj_kernel(x_ref, cos_ref, sin_ref, w1_ref, qg_ref, kvg_ref, wq_ref, eq_ref, ek_ref, wuk_ref,
                 wuvt_ref, vones_ref, lng_ref, lnb_ref, wmix_ref, bias_ref, *out_refs, chunk_len, with_kv, with_vln):
    out_refs = list(out_refs)
    ckv_ref, kr_ref, q_ref = out_refs[:3]
    rest = out_refs[3:]
    if with_kv:
        k_ref, vt_ref = rest[:2]
        rest = rest[2:]
    if with_vln:
        vln_ref = rest[0]
        rest = rest[1:]
    s_ref, xq_ref = rest

    tm = x_ref.shape[0]
    xb = x_ref[...].astype(BF16)
    h = _dot(xb, w1_ref[...])
    cos = cos_ref[...]
    sin = sin_ref[...]

    cq = _rms_norm(h[:, P_CQ:P_CKV], qg_ref[...]).astype(BF16)
    qq = _dot(cq, wq_ref[...])
    q_full = qq[:, :QK_SLAB]
    for j in range(ROPE_ALL // LANES):
        qa = qq[:, QK_SLAB + j * LANES:QK_SLAB + (j + 1) * LANES]
        qb = qq[:, QK_SLAB + ROPE_ALL + j * LANES:QK_SLAB + ROPE_ALL + (j + 1) * LANES]
        qr = (qa * cos + qb * sin).astype(BF16)
        q_full = q_full + _dot(qr, eq_ref[j * LANES:(j + 1) * LANES, :])
    q_ref[...] = q_full.astype(BF16)

    ckv = _rms_norm(h[:, P_CKV:P_KRA], kvg_ref[...])
    ckv_ref[...] = ckv
    kr128 = h[:, P_KRA:P_KRB] * cos + h[:, P_KRB:P_U] * sin
    kr_ref[...] = kr128[:, :QK_ROPE_DIM]
    if with_kv:
        ckv_b = ckv.astype(BF16)
        k_full = _dot(ckv_b, wuk_ref[...]) + _dot(kr128.astype(BF16), ek_ref[...])
        k_ref[...] = k_full.astype(BF16)
        vt_ref[...] = (_dot_nt(wuvt_ref[...], ckv_b) + vones_ref[...]).astype(BF16)

    u = jax.nn.gelu(h[:, P_U:P_V])
    vln = _layer_norm(jax.nn.gelu(h[:, P_V:P_XQ]), lng_ref[...], lnb_ref[...])
    if with_vln:
        vln_ref[...] = vln
    vb = vln.astype(BF16)
    row = lax.broadcasted_iota(jnp.int32, (CHUNK, CHUNK), 0)
    col = lax.broadcasted_iota(jnp.int32, (CHUNK, CHUNK), 1)
    mask = col <= row
    if chunk_len < CHUNK:
        mask = mask & ((col // chunk_len) == (row // chunk_len))
    wm = [jnp.where(mask, wmix_ref[g], jnp.zeros((), BF16)) for g in range(N_GROUPS)]
    for c in range(tm // CHUNK):
        rs = slice(c * CHUNK, (c + 1) * CHUNK)
        for g in range(N_GROUPS):
            cs = slice(g * GROUP_DIM, (g + 1) * GROUP_DIM)
            mixed = _dot(wm[g], vb[rs, cs]) + bias_ref[:, cs]
            s_ref[rs, cs] = (u[rs, cs] * mixed).astype(BF16)

    xq_ref[...] = h[:, P_XQ:P_END].astype(BF16)


def _proj_call(x, cos, sin, lw, *, layer, tm, chunk_len, with_kv, with_vln, name):
    rows = x.shape[0]
    grid = (rows // tm,)

    def row_spec(n):
        return pl.BlockSpec((tm, n), lambda i: (i, 0))

    def lspec(*shape):
        return _layer_spec(shape, layer)

    in_specs = [row_spec(D_MODEL), row_spec(LANES), row_spec(LANES),
                lspec(D_MODEL, P_END), lspec(1, Q_LORA_RANK), lspec(1, KV_LORA_RANK),
                lspec(Q_LORA_RANK, QK_SLAB + 2 * ROPE_ALL), _const_spec((ROPE_ALL, QK_SLAB)),
                _const_spec((LANES, QK_SLAB)), lspec(KV_LORA_RANK, QK_SLAB),
                lspec(VT_ALL, KV_LORA_RANK), _const_spec((VT_ALL, 1)), lspec(1, SGU_DIM),
                lspec(1, SGU_DIM), lspec(N_GROUPS, CHUNK, CHUNK), lspec(CHUNK, SGU_DIM)]
    out_shape = [jax.ShapeDtypeStruct((rows, KV_LORA_RANK), F32),
                 jax.ShapeDtypeStruct((rows, QK_ROPE_DIM), F32),
                 jax.ShapeDtypeStruct((rows, QK_SLAB), BF16)]
    out_specs = [row_spec(KV_LORA_RANK), row_spec(QK_ROPE_DIM), row_spec(QK_SLAB)]
    if with_kv:
        out_shape += [jax.ShapeDtypeStruct((rows, QK_SLAB), BF16),
                      jax.ShapeDtypeStruct((VT_ALL, rows), BF16)]
        out_specs += [row_spec(QK_SLAB), pl.BlockSpec((VT_ALL, tm), lambda i: (0, i))]
    if with_vln:
        out_shape += [jax.ShapeDtypeStruct((rows, SGU_DIM), F32)]
        out_specs += [row_spec(SGU_DIM)]
    out_shape += [jax.ShapeDtypeStruct((rows, SGU_DIM), BF16), jax.ShapeDtypeStruct((rows, MEM_DIM), BF16)]
    out_specs += [row_spec(SGU_DIM), row_spec(MEM_DIM)]

    body = functools.partial(_proj_kernel, chunk_len=chunk_len, with_kv=with_kv, with_vln=with_vln)
    return pl.pallas_call(
        body, grid=grid, in_specs=in_specs, out_specs=out_specs, out_shape=out_shape,
        compiler_params=_params(1), name=name,
    )(x, cos, sin, lw["w1"], lw["qg"], lw["kvg"], lw["wq"], lw["eq"], lw["ek"], lw["wuk"], lw["wuvt"],
      lw["vones"], lw["sgu_g"], lw["sgu_b"], lw["wmix"], lw["mixbias"])


def _memkv_kernel(mem_ref, wk_ref, wv_ref, mk_ref, mv_ref):
    mb = mem_ref[...].astype(BF16)
    mk_ref[...] = _dot(mb, wk_ref[...])
    mv_ref[...] = _dot(mb, wv_ref[...])


def _memkv_call(mem, wk, wv):
    depth = wk.shape[0]
    w_spec = pl.BlockSpec((None, D_MODEL, MEM_DIM), lambda l: (l, 0, 0))
    o_spec = pl.BlockSpec((None, N_MEM, MEM_DIM), lambda l: (l, 0, 0))
    return pl.pallas_call(
        _memkv_kernel, grid=(depth,),
        in_specs=[pl.BlockSpec((N_MEM, D_MODEL), lambda l: (0, 0)), w_spec, w_spec],
        out_specs=[o_spec, o_spec],
        out_shape=[jax.ShapeDtypeStruct((depth, N_MEM, MEM_DIM), F32)] * 2,
        compiler_params=_params(1), name="mem_kv",
    )(mem, wk, wv)


def _memattn_kernel(xq_ref, mk_ref, mv_ref, o_ref, *, bt):
    pairs = [(b, hh) for b in range(bt) for hh in range(MEM_HEADS)]

    def head_rows(ref, b, hh):
        return ref[b, pl.ds(hh, N_MEM, stride=MEM_HEADS), :].astype(BF16)

    def cols(hh):
        return slice(hh * MEM_HEAD_DIM, (hh + 1) * MEM_HEAD_DIM)

    scores = [_dot_nt(xq_ref[b, :, cols(hh)], head_rows(mk_ref, b, hh)) * MEM_SCALE for b, hh in pairs]
    probs = []
    for s in scores:
        e = jnp.exp(s - jnp.max(s, axis=-1, keepdims=True))
        probs.append((e / jnp.sum(e, axis=-1, keepdims=True)).astype(BF16))
    for (b, hh), p in zip(pairs, probs):
        o_ref[b, :, cols(hh)] = _dot(p, head_rows(mv_ref, b, hh)).astype(BF16)


def _memattn_call(xq, mk, mv, *, layer, bt, ts, name):
    nb, s, _ = xq.shape
    kv_spec = pl.BlockSpec((None, bt, N_MEM * MEM_HEADS, MEM_HEAD_DIM), lambda b, i: (layer, b, 0, 0))
    x_spec = pl.BlockSpec((bt, ts, MEM_DIM), lambda b, i: (b, i, 0))
    return pl.pallas_call(
        functools.partial(_memattn_kernel, bt=bt), grid=(nb // bt, s // ts),
        in_specs=[x_spec, kv_spec, kv_spec], out_specs=x_spec,
        out_shape=jax.ShapeDtypeStruct((nb, s, MEM_DIM), BF16),
        compiler_params=_params(2), name=name,
    )(xq, mk, mv)


def _flash_kernel(qtab_ref, ktab_ref, q_ref, k_ref, vt_ref, o_ref, m_ref, acc_ref, s_ref, *, tq, nsub):
    tk = tq
    t = pl.program_id(0)
    qi = qtab_ref[t]
    ki = ktab_ref[t]

    @pl.when(ki == 0)
    def _():
        m_ref[...] = jnp.full(m_ref.shape, -jnp.inf, F32)
        acc_ref[...] = jnp.zeros(acc_ref.shape, F32)

    def sub_block(sub, masked):
        k0 = pl.multiple_of(sub * tk, tk)

        def scores(hh):
            cs = slice(hh * HEAD_SLAB, (hh + 1) * HEAD_SLAB)
            return _dot_nt(k_ref[pl.ds(k0, tk), cs], q_ref[:, cs])

        keep = None
        if masked:
            key = lax.broadcasted_iota(jnp.int32, (tk, tq), 0)
            qry = lax.broadcasted_iota(jnp.int32, (tk, tq), 1)
            keep = key <= qry

        def update(hh, st):
            if keep is not None:
                st = jnp.where(keep, st, -jnp.inf)
            s_ref[hh % 2] = st
            s3 = s_ref[hh % 2].reshape(tk // SUBLANES, SUBLANES, tq)
            m_prev = m_ref[hh]
            m_cur = jnp.max(jnp.max(s3, axis=0), axis=0, keepdims=True)
            m_new = jnp.maximum(m_prev, m_cur)
            alpha = jnp.exp2((m_prev - m_new) * EXP2_SCALE)
            p = jnp.exp2((s3 - m_new[None]) * EXP2_SCALE).reshape(tk, tq).astype(BF16)
            rs = slice(hh * VT_ROWS, (hh + 1) * VT_ROWS)
            acc_ref[rs, :] = alpha[0:1] * acc_ref[rs, :] + _dot(vt_ref[rs, pl.ds(k0, tk)], p)
            m_ref[hh] = m_new

        pending = {0: scores(0), 1: scores(1)}
        for hh in range(N_HEADS):
            if hh + 2 < N_HEADS:
                pending[hh + 2] = scores(hh + 2)
            update(hh, pending.pop(hh))

    n_full = jnp.clip(qi - ki * nsub, 0, nsub)

    def full_body(sub, carry):
        sub_block(sub, False)
        return carry

    lax.fori_loop(0, n_full, full_body, 0)

    @pl.when(n_full < nsub)
    def _():
        sub_block(n_full, True)
        outs = []
        for hh in range(N_HEADS):
            base = hh * VT_ROWS
            inv = 1.0 / acc_ref[base + V_HEAD_DIM:base + V_HEAD_DIM + 1, :]
            outs.append(acc_ref[base:base + V_HEAD_DIM, :] * inv)
        o_ref[...] = jnp.concatenate(outs, axis=0).T.astype(BF16)


def _flash_call(q, k, vt, *, tq, nsub, name):
    s = q.shape[0]
    tkb = tq * nsub
    assert s % tkb == 0
    steps =[(qi, ki) for qi in range(s // tq) for ki in range(qi // nsub + 1)]
    qtab = jnp.asarray(np.array([p[0] for p in steps], np.int32))
    ktab = jnp.asarray(np.array([p[1] for p in steps], np.int32))
    width = N_HEADS * V_HEAD_DIM
    grid_spec = pltpu.PrefetchScalarGridSpec(
        num_scalar_prefetch=2, grid=(len(steps),),
        in_specs=[pl.BlockSpec((tq, QK_SLAB), lambda t, qt, kt: (qt[t], 0)),
                  pl.BlockSpec((tkb, QK_SLAB), lambda t, qt, kt: (kt[t], 0)),
                  pl.BlockSpec((VT_ALL, tkb), lambda t, qt, kt: (0, kt[t]))],
        out_specs=pl.BlockSpec((tq, width), lambda t, qt, kt: (qt[t], 0)),
        scratch_shapes=[pltpu.VMEM((N_HEADS, SUBLANES, tq), F32), pltpu.VMEM((VT_ALL, tq), F32),
                        pltpu.VMEM((2, tq, tq), F32)])
    return pl.pallas_call(
        functools.partial(_flash_kernel, tq=tq, nsub=nsub), grid_spec=grid_spec,
        out_shape=jax.ShapeDtypeStruct((s, width), BF16),
        compiler_params=_params(1), name=name,
    )(qtab, ktab, q, k, vt)


def _qcat_kernel(q_ref, w_ref, o_ref):
    for hh in range(N_HEADS):
        o_ref[:, hh, :] = _dot(q_ref[:, hh * HEAD_SLAB:(hh + 1) * HEAD_SLAB], w_ref[hh])


def _qcat_call(q, wcat, *, layer, name):
    rows = q.shape[0]
    return pl.pallas_call(
        _qcat_kernel, grid=(1,),
        in_specs=[pl.BlockSpec((rows, QK_SLAB), lambda i: (0, 0)),
                  pl.BlockSpec((None, N_HEADS, HEAD_SLAB, QCAT), lambda i: (layer, 0, 0, 0))],
        out_specs=pl.BlockSpec((rows, N_HEADS, QCAT), lambda i: (0, 0, 0)),
        out_shape=jax.ShapeDtypeStruct((rows, N_HEADS, QCAT), F32),
        compiler_params=_params(1), name=name,
    )(q, wcat)


def _decode_kernel(pt_ref, q_ref, cnew_ref, krnew_ref, cache_c_ref, cache_krt_ref, o_ref,
                   cbuf, krbuf, csem, krsem, m_ref, l_ref, acc_ref, *, layer, n_pages_step, steps_per_batch, n_split, n_ahead, n_new):
    g = n_pages_step
    page = cbuf.shape[1] // g
    t = pl.program_id(0)
    j = t % steps_per_batch
    slot = t % 2
    nq = q_ref.shape[1]

    def page_copies(step, buf):
        b = step // steps_per_batch
        first = (step % steps_per_batch) * g
        copies = []
        for i in range(g):
            pg = pt_ref[b, first + i]
            copies.append(pltpu.make_async_copy(
                cache_c_ref.at[layer, pg], cbuf.at[buf, pl.ds(i * page, page), :], csem.at[buf]))
            copies.append(pltpu.make_async_copy(
                cache_krt_ref.at[layer, pg], krbuf.at[buf, :, pl.ds(i * page, page)], krsem.at[buf]))
        return copies

    @pl.when(t == 0)
    def _():
        for cp in page_copies(t, slot):
            cp.start()

    @pl.when(t + 1 < pl.num_programs(0))
    def _():
        for cp in page_copies(t + 1, 1 - slot):
            cp.start()

    @pl.when(j == 0)
    def _():
        m_ref[...] = jnp.full(m_ref.shape, -jnp.inf, F32)
        l_ref[...] = jnp.zeros(l_ref.shape, F32)
        acc_ref[...] = jnp.zeros(acc_ref.shape, F32)

    q = q_ref[0]
    ql = q[:, :KV_LORA_RANK].astype(BF16)
    qr = q[:, KV_LORA_RANK:KV_LORA_RANK + QK_ROPE_DIM].astype(BF16)

    for cp in page_copies(t, slot):
        cp.wait()

    keys = (g * page) // n_split

    def chunk_scores(c):
        kb = cbuf[slot, pl.ds(c * keys, keys), :].astype(BF16)
        krt = krbuf[slot, :, pl.ds(c * keys, keys)].astype(BF16)
        return kb, (_dot_nt(ql, kb) + _dot(qr, krt)) * MLA_SCALE

    m_run = m_ref[...]
    l_run = l_ref[...]
    acc = acc_ref[...]
    pending = {c: chunk_scores(c) for c in range(min(n_ahead, n_split))}
    for c in range(n_split):
        if c + n_ahead < n_split:
            pending[c + n_ahead] = chunk_scores(c + n_ahead)
        kb, s = pending.pop(c)
        m_new = jnp.maximum(m_run, jnp.max(s, axis=-1, keepdims=True))
        alpha = jnp.exp(m_run - m_new)
        p = jnp.exp(s - m_new)
        l_run = alpha * l_run + jnp.sum(p, axis=-1, keepdims=True)
        acc = alpha * acc + _dot(p.astype(BF16), kb)
        m_run = m_new
    m_ref[...] = m_run
    l_ref[...] = l_run
    acc_ref[...] = acc

    @pl.when(j == steps_per_batch - 1)
    def _():
        qlf = ql.astype(F32)
        qrf = qr.astype(F32)
        tok = lax.broadcasted_iota(jnp.int32, (nq, 1), 0) // N_HEADS
        s_new = []
        cn = []
        for t in range(n_new):
            c_t = cnew_ref[0, t:t + 1, :].astype(BF16).astype(F32)
            kr_t = krnew_ref[0, t:t + 1, :].astype(BF16).astype(F32)
            s_t = (jnp.sum(qlf * c_t, axis=-1, keepdims=True)
                   + jnp.sum(qrf * kr_t, axis=-1, keepdims=True)) * MLA_SCALE
            s_new.append(jnp.where(tok >= t, s_t, -jnp.inf))
            cn.append(c_t)
        m_prev = m_ref[...]
        m_fin = m_prev
        for s_t in s_new:
            m_fin = jnp.maximum(m_fin, s_t)
        alpha = jnp.exp(m_prev - m_fin)
        l_fin = alpha * l_ref[...]
        acc = alpha * acc_ref[...]
        for t in range(n_new):
            p_t = jnp.exp(s_new[t] - m_fin)
            l_fin = l_fin + p_t
            acc = acc + p_t.astype(BF16).astype(F32) * cn[t]
        o_ref[0] = acc * (1.0 / l_fin)


def _decode_call(page_table, qcat, c_new, kr_new, cache_c, cache_krt, *, layer, pages_per_step, n_split, n_ahead,
                 name):
    nb, nq, _ = qcat.shape
    n_pages = page_table.shape[1]
    page = cache_c.shape[2]
    g = pages_per_step
    spb = n_pages // g
    n_new = c_new.shape[1]
    in_specs = [pl.BlockSpec((1, nq, QCAT), lambda t, pt: (t // spb, 0, 0)),
                pl.BlockSpec((1, n_new, KV_LORA_RANK), lambda t, pt: (t // spb, 0, 0)),
                pl.BlockSpec((1, n_new, QK_ROPE_DIM), lambda t, pt: (t // spb, 0, 0)),
                pl.BlockSpec(memory_space=pl.ANY), pl.BlockSpec(memory_space=pl.ANY)]
    grid_spec = pltpu.PrefetchScalarGridSpec(
        num_scalar_prefetch=1, grid=(nb * spb,), in_specs=in_specs,
        out_specs=pl.BlockSpec((1, nq, KV_LORA_RANK), lambda t, pt: (t // spb, 0, 0)),
        scratch_shapes=[pltpu.VMEM((2, g * page, KV_LORA_RANK), F32), pltpu.VMEM((2, QK_ROPE_DIM, g * page), F32),
                        pltpu.SemaphoreType.DMA((2,)), pltpu.SemaphoreType.DMA((2,)),
                        pltpu.VMEM((nq, 1), F32), pltpu.VMEM((nq, 1), F32), pltpu.VMEM((nq, KV_LORA_RANK), F32)])
    return pl.pallas_call(
        functools.partial(_decode_kernel, layer=layer, n_pages_step=g, steps_per_batch=spb, n_split=n_split, n_ahead=n_ahead,
                          n_new=n_new),
        grid_spec=grid_spec, out_shape=jax.ShapeDtypeStruct((nb, nq, KV_LORA_RANK), F32),
        compiler_params=_params(1), name=name,
    )(page_table, qcat, c_new, kr_new, cache_c, cache_krt)


def _vup_kernel(o_ref, w_ref, a_ref):
    acc = _dot(o_ref[:, 0, :].astype(BF16), w_ref[0])
    for hh in range(1, N_HEADS):
        acc = acc + _dot(o_ref[:, hh, :].astype(BF16), w_ref[hh])
    a_ref[...] = acc.astype(BF16)


def _vup_call(o_lat, wuv_exp, *, layer, name):
    rows = o_lat.shape[0]
    width = N_HEADS * V_HEAD_DIM
    return pl.pallas_call(
        _vup_kernel, grid=(1,),
        in_specs=[pl.BlockSpec((rows, N_HEADS, KV_LORA_RANK), lambda i: (0, 0, 0)),
                  pl.BlockSpec((None, N_HEADS, KV_LORA_RANK, width), lambda i: (layer, 0, 0, 0))],
        out_specs=pl.BlockSpec((rows, width), lambda i: (0, 0)),
        out_shape=jax.ShapeDtypeStruct((rows, width), BF16),
        compiler_params=_params(1), name=name,
    )(o_lat, wuv_exp)


def _merge_kernel(x_ref, a_ref, s_ref, m_ref, wg_ref, bg_ref, wbr_ref, wo_ref, g_ref, b_ref, o_ref, *, alpha):
    xf = x_ref[...]
    xb = xf.astype(BF16)
    merged = None
    for i, br in enumerate((a_ref, s_ref, m_ref)):
        gate = jax.nn.sigmoid(_dot(xb, wg_ref[:, i * D_MODEL:(i + 1) * D_MODEL]) + bg_ref[i:i + 1, :])
        term = gate * _dot(br[...], wbr_ref[i])
        merged = term if merged is None else merged + term
    y = alpha * xf + _dot(merged.astype(BF16), wo_ref[...])
    o_ref[...] = _layer_norm(y, g_ref[...], b_ref[...])


def _merge_call(x, a, s, m, lw, *, layer, tm, alpha, name):
    rows = x.shape[0]

    def row_spec(n):
        return pl.BlockSpec((tm, n), lambda i: (i, 0))

    def lspec(*shape):
        return _layer_spec(shape, layer)

    return pl.pallas_call(
        functools.partial(_merge_kernel, alpha=alpha), grid=(rows // tm,),
        in_specs=[row_spec(D_MODEL), row_spec(BRANCH_DIM), row_spec(BRANCH_DIM), row_spec(BRANCH_DIM),
                  lspec(D_MODEL, N_BRANCH * D_MODEL), lspec(N_BRANCH, D_MODEL),
                  lspec(N_BRANCH, BRANCH_DIM, D_MODEL), lspec(D_MODEL, D_MODEL),
                  lspec(1, D_MODEL), lspec(1, D_MODEL)],
        out_specs=row_spec(D_MODEL),
        out_shape=jax.ShapeDtypeStruct((rows, D_MODEL), F32),
        compiler_params=_params(1), name=name,
    )(x, a, s, m, lw["wg"], lw["bg"], lw["wbr"], lw["wo"], lw["ln1_g"], lw["ln1_b"])


def _ffn_kernel(x_ref, w1_ref, w3_ref, w2_ref, g_ref, b_ref, o_ref, *, alpha, n_split):
    xf = x_ref[...]
    xb = xf.astype(BF16)
    d_ff = w1_ref.shape[1]
    step = d_ff // n_split
    ff = None
    for c in range(n_split):
        cs = slice(c * step, (c + 1) * step)
        act = (jax.nn.silu(_dot(xb, w1_ref[:, cs])) * _dot(xb, w3_ref[:, cs])).astype(BF16)
        part = _dot(act, w2_ref[cs, :])
        ff = part if ff is None else ff + part
    o_ref[...] = _layer_norm(alpha * xf + ff, g_ref[...], b_ref[...])


def _ffn_call(x, lw, *, layer, tm, alpha, name):
    rows = x.shape[0]
    d_ff = lw["w_ff1"].shape[2]
    row_spec = pl.BlockSpec((tm, D_MODEL), lambda i: (i, 0))

    def lspec(*shape):
        return _layer_spec(shape, layer)

    return pl.pallas_call(
        functools.partial(_ffn_kernel, alpha=alpha, n_split=2), grid=(rows // tm,),
        in_specs=[row_spec, lspec(D_MODEL, d_ff), lspec(D_MODEL, d_ff),
                  lspec(d_ff, D_MODEL), lspec(1, D_MODEL), lspec(1, D_MODEL)],
        out_specs=row_spec,
        out_shape=jax.ShapeDtypeStruct((rows, D_MODEL), F32),
        compiler_params=_params(1), name=name,
    )(x, lw["w_ff1"], lw["w_ff3"], lw["w_ff2"], lw["ln2_g"], lw["ln2_b"])


def _rot_cols(w):
    half = w.shape[-1] // 2
    return jnp.concatenate([-w[..., half:], w[..., :half]], axis=-1)


def _placement(rows, cols, pairs):
    e = np.zeros((rows, cols), np.float32)
    for r, c in pairs:
        e[r, c] = 1.0
    return jnp.asarray(e, dtype=BF16)


def _prep_weights(chunk_lens, w_in, b_gate, q_norm_g, kv_norm_g, w_uq, w_uk, w_uv, sgu_ln_g, sgu_ln_b, w_s, b_s,
                  w_br, w_o, ln1_g, ln1_b, w_ff1, w_ff3, w_ff2, ln2_g, ln2_b):
    depth = w_in.shape[0]
    wkr = w_in[:, :, OFF_KR:OFF_U]
    zpad = jnp.zeros((depth, D_MODEL, LANES - QK_ROPE_DIM), F32)
    w1 = jnp.concatenate([w_in[:, :, OFF_CQ:OFF_KR], wkr, zpad, _rot_cols(wkr), zpad, w_in[:, :, OFF_U:OFF_G]],
                         axis=2)

    wuq = w_uq.reshape(depth, Q_LORA_RANK, N_HEADS, QK_HEAD_DIM)
    wq_nope = jnp.pad(wuq[..., :QK_NOPE_DIM], ((0, 0), (0, 0), (0, 0), (0, HEAD_SLAB - QK_NOPE_DIM)))
    wq_rope = wuq[..., QK_NOPE_DIM:]
    wq = jnp.concatenate([wq_nope.reshape(depth, Q_LORA_RANK, QK_SLAB),
                          wq_rope.reshape(depth, Q_LORA_RANK, ROPE_ALL),
                          _rot_cols(wq_rope).reshape(depth, Q_LORA_RANK, ROPE_ALL)], axis=2)
    eq = _placement(ROPE_ALL, QK_SLAB, [(hh * QK_ROPE_DIM + d, hh * HEAD_SLAB + QK_NOPE_DIM + d)
                                        for hh in range(N_HEADS) for d in range(QK_ROPE_DIM)])
    ek = _placement(LANES, QK_SLAB, [(d, hh * HEAD_SLAB + QK_NOPE_DIM + d)
                                     for hh in range(N_HEADS) for d in range(QK_ROPE_DIM)])
    wuk = jnp.pad(w_uk, ((0, 0), (0, 0), (0, 0), (0, HEAD_SLAB - QK_NOPE_DIM))).reshape(depth, KV_LORA_RANK, QK_SLAB)
    wuvt = jnp.pad(jnp.transpose(w_uv, (0, 2, 3, 1)), ((0, 0), (0, 0), (0, VT_ROWS - V_HEAD_DIM), (0, 0)))
    wuvt = wuvt.reshape(depth, VT_ALL, KV_LORA_RANK)
    vones = np.zeros((VT_ALL, 1), np.float32)
    vones[np.arange(N_HEADS) * VT_ROWS + V_HEAD_DIM] = 1.0

    wcat_top = jnp.pad(jnp.transpose(w_uk, (0, 2, 3, 1)), ((0, 0), (0, 0), (0, 0), (0, QCAT - KV_LORA_RANK)))
    rope_rows = np.zeros((HEAD_SLAB - QK_NOPE_DIM, QCAT), np.float32)
    rope_rows[np.arange(QK_ROPE_DIM), KV_LORA_RANK + np.arange(QK_ROPE_DIM)] = 1.0
    wcat = jnp.concatenate(
        [wcat_top, jnp.broadcast_to(jnp.asarray(rope_rows), (depth, N_HEADS) + rope_rows.shape)], axis=2)
    eye = jnp.eye(N_HEADS, dtype=F32)
    wuv_exp = (jnp.transpose(w_uv, (0, 2, 1, 3))[:, :, :, None, :] * eye[None, :, None, :, None])
    wuv_exp = wuv_exp.reshape(depth, N_HEADS, KV_LORA_RANK, N_HEADS * V_HEAD_DIM)

    def row(v):
        return v[:, None, :]

    lw = {
        "w1": w1.astype(BF16), "qg": row(q_norm_g), "kvg": row(kv_norm_g),
        "wq": wq.astype(BF16), "eq": eq, "ek": ek, "wuk": wuk.astype(BF16), "wuvt": wuvt.astype(BF16),
        "vones": jnp.asarray(vones),
        "sgu_g": row(sgu_ln_g), "sgu_b": row(sgu_ln_b),
        "wcat": wcat.astype(BF16), "wuv_exp": wuv_exp.astype(BF16),
        "wg": w_in[:, :, OFF_G:].astype(BF16), "bg": b_gate, "wbr": w_br.astype(BF16), "wo": w_o.astype(BF16),
        "ln1_g": row(ln1_g), "ln1_b": row(ln1_b),
        "w_ff1": w_ff1.astype(BF16), "w_ff3": w_ff3.astype(BF16), "w_ff2": w_ff2.astype(BF16),
        "ln2_g": row(ln2_g), "ln2_b": row(ln2_b),
    }
    for key, cl in chunk_lens.items():
        reps = CHUNK // cl
        lw["wmix_" + key] = jnp.tile(w_s[:, :, :cl, :cl], (1, 1, reps, reps)).astype(BF16)
        bias = jnp.tile(jnp.transpose(b_s[:, :, :cl], (0, 2, 1)), (1, reps, 1))
        lw["mixbias_" + key] = jnp.repeat(bias, GROUP_DIM, axis=2)
    return lw


def _rope_tables(pos):
    half = QK_ROPE_DIM // 2
    inv_freq = jnp.power(ROPE_THETA, -jnp.arange(half, dtype=F32) / half)
    ang = pos.astype(F32)[:, None] * inv_freq[None, :]
    reps = LANES // half
    return jnp.tile(jnp.cos(ang), (1, reps)), jnp.tile(jnp.sin(ang), (1, reps))


def kernel(x_prompt, x_sample, mem_prompt, cache_kv_latent, cache_k_rope, cache_mem_k, cache_mem_v, page_table,
           w_in, b_gate, q_norm_g, kv_norm_g, w_uq, w_uk, w_uv, sgu_ln_g, sgu_ln_b, w_s, b_s, w_mk, w_mv,
           w_br, w_o, ln1_g, ln1_b, w_ff1, w_ff3, w_ff2, ln2_g, ln2_b):
    nbp, seq, _ = x_prompt.shape
    nbs, tdec, _ = x_sample.shape
    depth = w_in.shape[0]
    n_pages = page_table.shape[1]
    past = n_pages * cache_kv_latent.shape[2]
    alpha = (2 * depth) ** 0.25
    assert nbp == 1 and seq % CHUNK == 0 and tdec < CHUNK and CHUNK % tdec == 0

    cos_p, sin_p = _rope_tables(jnp.arange(seq, dtype=jnp.int32))
    cos_s, sin_s = _rope_tables(jnp.tile(past + jnp.arange(tdec, dtype=jnp.int32), nbs))
    chunk_lens = {"p": min(seq, CHUNK), "s": min(tdec, CHUNK)}

    mk_p_all, mv_p_all = _memkv_call(mem_prompt[0], w_mk.astype(BF16), w_mv.astype(BF16))
    mk_p4 = mk_p_all.reshape(depth, nbp, N_MEM * MEM_HEADS, MEM_HEAD_DIM)
    mv_p4 = mv_p_all.reshape(depth, nbp, N_MEM * MEM_HEADS, MEM_HEAD_DIM)
    cache_krt = jnp.swapaxes(cache_k_rope, 2, 3)
    cache_mk = cache_mem_k.reshape(depth, nbs, N_MEM * MEM_HEADS, MEM_HEAD_DIM)
    cache_mv = cache_mem_v.reshape(depth, nbs, N_MEM * MEM_HEADS, MEM_HEAD_DIM)

    xp = x_prompt.reshape(seq, D_MODEL)
    xs = x_sample.reshape(nbs * tdec, D_MODEL)
    ckv_p, kr_p, ckv_s, kr_s, v_s = [], [], [], [], []
    lw = _prep_weights(chunk_lens, w_in, b_gate, q_norm_g, kv_norm_g, w_uq, w_uk, w_uv, sgu_ln_g, sgu_ln_b,
                       w_s, b_s, w_br, w_o, ln1_g, ln1_b, w_ff1, w_ff3, w_ff2, ln2_g, ln2_b)
    lw_p = dict(lw, wmix=lw["wmix_p"], mixbias=lw["mixbias_p"])
    lw_s = dict(lw, wmix=lw["wmix_s"], mixbias=lw["mixbias_s"])
    for l in range(depth):
        c_new, k_new, q, k, vt, s_out, xq = _proj_call(
            xp, cos_p, sin_p, lw_p, layer=l, tm=512, chunk_len=chunk_lens["p"], with_kv=True, with_vln=False,
            name=f"proj_prompt_{l}")
        ckv_p.append(c_new)
        kr_p.append(k_new)
        a_out = _flash_call(q, k, vt, tq=512, nsub=4, name=f"flash_prompt_{l}")
        m_out = _memattn_call(xq[None], mk_p4, mv_p4, layer=l, bt=1, ts=512, name=f"memattn_prompt_{l}")[0]
        x1 = _merge_call(xp, a_out, s_out, m_out, lw, layer=l, tm=512, alpha=alpha, name=f"merge_prompt_{l}")
        xp = _ffn_call(x1, lw, layer=l, tm=512, alpha=alpha, name=f"ffn_prompt_{l}")

        c_new, k_new, q, vln, s_out, xq = _proj_call(
            xs, cos_s, sin_s, lw_s, layer=l, tm=256, chunk_len=chunk_lens["s"], with_kv=False, with_vln=True,
            name=f"proj_sample_{l}")
        ckv_s.append(c_new)
        kr_s.append(k_new)
        v_s.append(vln)
        qcat = _qcat_call(q, lw["wcat"], layer=l, name=f"qcat_sample_{l}")
        o_lat = _decode_call(page_table, qcat.reshape(nbs, tdec * N_HEADS, QCAT),
                             c_new.reshape(nbs, tdec, KV_LORA_RANK), k_new.reshape(nbs, tdec, QK_ROPE_DIM),
                             cache_kv_latent, cache_krt, layer=l, pages_per_step=128, n_split=8, n_ahead=4,
                             name=f"decode_sample_{l}")
        a_out = _vup_call(o_lat.reshape(nbs * tdec, N_HEADS, KV_LORA_RANK), lw["wuv_exp"], layer=l,
                          name=f"vup_sample_{l}")
        m_out = _memattn_call(xq.reshape(nbs, tdec, MEM_DIM), cache_mk, cache_mv, layer=l, bt=8, ts=tdec,
                              name=f"memattn_sample_{l}").reshape(nbs * tdec, MEM_DIM)
        x1 = _merge_call(xs, a_out, s_out, m_out, lw, layer=l, tm=256, alpha=alpha, name=f"merge_sample_{l}")
        xs = _ffn_call(x1, lw, layer=l, tm=256, alpha=alpha, name=f"ffn_sample_{l}")

    return (xp.reshape(nbp, seq, D_MODEL),
            xs.reshape(nbs, tdec, D_MODEL),
            jnp.stack(ckv_p).reshape(depth, nbp, seq, KV_LORA_RANK),
            jnp.stack(kr_p).reshape(depth, nbp, seq, QK_ROPE_DIM),
            mk_p_all.reshape(depth, nbp, N_MEM, MEM_HEADS, MEM_HEAD_DIM),
            mv_p_all.reshape(depth, nbp, N_MEM, MEM_HEADS, MEM_HEAD_DIM),
            jnp.stack(ckv_s).reshape(depth, nbs, tdec, KV_LORA_RANK),
            jnp.stack(kr_s).reshape(depth, nbs, tdec, QK_ROPE_DIM),
            jnp.stack(v_s).reshape(depth, nbs, tdec, SGU_DIM))
```
